```python
import math
import jax
import jax.numpy as jnp
from jax import lax
import numpy as np

D_MODEL = 4096
BATCH = 4
SEQ = 2048
DEPTH = 1
DEC_BATCH = 128
DEC_SEQ = 1
PAST_LEN = 16384
PAGE_SIZE = 128

V_HEAD = 128
N_MLA_HEADS = (D_MODEL // 2) // V_HEAD
Q_LORA = D_MODEL // 4
KV_LORA = 512
QK_NOPE = 128
QK_ROPE = 64
ROPE_THETA = 10000.0
MLA_SCALE = (QK_NOPE + QK_ROPE) ** -0.5
DIFF_HEAD = 128
DIFF_V = 2 * DIFF_HEAD
N_DIFF_HEADS = (D_MODEL // 2) // DIFF_V
DIFF_SCALE = DIFF_HEAD ** -0.5
SUBLN_EPS = 1e-5
MIX_WIDTH = N_MLA_HEADS * V_HEAD + N_DIFF_HEADS * DIFF_V
IN_WIDTH = Q_LORA + KV_LORA + QK_ROPE + 2 * N_DIFF_HEADS * DIFF_HEAD + 2 * DIFF_HEAD + DIFF_V
N_BUCKETS = 32
MAX_DISTANCE = 128
N_EXPERTS = 32
TOP_K = 4
D_EXPERT = D_MODEL // 2
SWIGLU_LIMIT = 7.0
SWIGLU_ALPHA = 1.702
DN_ALPHA = (2 * DEPTH) ** 0.25
DN_BETA = (8 * DEPTH) ** -0.25
LN_EPS = 1e-5
RMS_EPS = 1e-6
Q_BLOCK = 128

kernel_name = 'hybrid_mla_diffattn_moe_step'


def _rmsnorm(x, g, eps):
    xf = x.astype(jnp.float32)
    y = xf * lax.rsqrt(jnp.mean(xf * xf, axis=-1, keepdims=True) + eps) * g.astype(jnp.float32)
    return y.astype(x.dtype)


def _layernorm(x, g, b):
    xf = x.astype(jnp.float32)
    mu = jnp.mean(xf, axis=-1, keepdims=True)
    var = jnp.mean(jnp.square(xf - mu), axis=-1, keepdims=True)
    return (xf - mu) * lax.rsqrt(var + LN_EPS) * g.astype(jnp.float32) + b.astype(jnp.float32)


def _rope(x, pos):
    inv = ROPE_THETA ** (-jnp.arange(0, QK_ROPE, 2, dtype=jnp.float32) / QK_ROPE)
    ang = pos.astype(jnp.float32)[:, None] * inv[None, :]
    ang = ang.reshape(ang.shape[:1] + (1,) * (x.ndim - 3) + ang.shape[1:])
    cos, sin = jnp.cos(ang), jnp.sin(ang)
    xf = x.astype(jnp.float32)
    x1, x2 = xf[..., : QK_ROPE // 2], xf[..., QK_ROPE // 2:]
    return jnp.concatenate([x1 * cos - x2 * sin, x2 * cos + x1 * sin], axis=-1).astype(x.dtype)


def _t5_bias(rel_bias, q_pos, k_pos):
    n = jnp.maximum(q_pos[:, None] - k_pos[None, :], 0)
    max_exact = N_BUCKETS // 2
    large = max_exact + (jnp.log(jnp.maximum(n, max_exact).astype(jnp.float32) / max_exact)
                         / math.log(MAX_DISTANCE / max_exact) * (N_BUCKETS - max_exact)).astype(jnp.int32)
    bucket = jnp.where(n < max_exact, n, jnp.minimum(large, N_BUCKETS - 1))
    return jnp.moveaxis(rel_bias[bucket].astype(jnp.float32), -1, 0)


def _project(x, pos, w_in, g_q, w_uq, g_kv, w_ukv):
    b, s, _ = x.shape
    h = x @ w_in
    i1 = Q_LORA
    i2 = i1 + KV_LORA + QK_ROPE
    i3 = i2 + 2 * N_DIFF_HEADS * DIFF_HEAD
    i4 = i3 + 2 * DIFF_HEAD
    hq, hkv, hdq, hdk, hdv = h[..., :i1], h[..., i1:i2], h[..., i2:i3], h[..., i3:i4], h[..., i4:]
    q = (_rmsnorm(hq, g_q, RMS_EPS) @ w_uq).reshape(b, s, N_MLA_HEADS, QK_NOPE + QK_ROPE)
    q_lat = jnp.einsum('bshd,chd->bhsc', q[..., :QK_NOPE], w_ukv[..., :QK_NOPE])
    q_pe = _rope(q[..., QK_NOPE:], pos).transpose(0, 2, 1, 3)
    ckv = _rmsnorm(hkv[..., :KV_LORA], g_kv, RMS_EPS)
    kpe = _rope(hkv[..., KV_LORA:], pos)
    qd = hdq.reshape(b, s, 2 * N_DIFF_HEADS, DIFF_HEAD).transpose(0, 2, 1, 3)
    dk = hdk.reshape(b, s, 2, DIFF_HEAD)
    return (q_lat, q_pe, qd), (ckv, kpe, dk, hdv)


def _mla_logits(q_lat, q_pe, ckv, kpe):
    s = jnp.einsum('bhqc,bkc->bhqk', q_lat, ckv) + jnp.einsum('bhqr,bkr->bhqk', q_pe, kpe)
    return s.astype(jnp.float32) * MLA_SCALE


def _diff_logits(qd, dk, bias):
    b, m, q, d = qd.shape
    s = jnp.einsum('bjhqd,bkjd->bjhqk', qd.reshape(b, 2, m // 2, q, d), dk)
    return (s.astype(jnp.float32) * DIFF_SCALE + bias).reshape(b, m, q, -1)


def _sm_init(b, g, q, c):
    return (jnp.full((b, g, q, 1), -jnp.inf, jnp.float32),
            jnp.zeros((b, g, q, 1), jnp.float32),
            jnp.zeros((b, g, q, c), jnp.float32))


def _sm_update(state, logits, values, mask=None):
    m, l, acc = state
    if mask is not None:
        logits = jnp.where(mask, logits, -jnp.inf)
    m_new = jnp.maximum(m, jnp.max(logits, axis=-1, keepdims=True))
    corr = jnp.exp(m - m_new)
    p = jnp.exp(logits - m_new)
    return (m_new,
            l * corr + jnp.sum(p, axis=-1, keepdims=True),
            acc * corr + jnp.einsum('bgqk,bkc->bgqc', p, values.astype(jnp.float32)))


def _merge_groups(st_mla, st_diff, w_ukv, lam, lam_init, g_sub):
    o_lat = st_mla[2] / st_mla[1]
    b, _, q, _ = o_lat.shape
    o_mla = jnp.einsum('bhqc,chd->bqhd', o_lat, w_ukv[..., QK_NOPE:].astype(jnp.float32)).reshape(b, q, -1)
    o = (st_diff[2] / st_diff[1]).reshape(b, 2, N_DIFF_HEADS, q, DIFF_V)
    o = _rmsnorm(o[:, 0] - lam * o[:, 1], g_sub, SUBLN_EPS) * (1.0 - lam_init)
    o_diff = o.transpose(0, 2, 1, 3).reshape(b, q, -1)
    return jnp.concatenate([o_mla, o_diff], axis=-1)


def _prompt_mix(q, kv, pos, rel_bias, w_ukv, lam, lam_init, g_sub):
    q_lat, q_pe, qd = q
    ckv, kpe, dk, dv = kv
    b, h, s, _ = q_lat.shape
    nb = s // Q_BLOCK

    def blocks(a):
        return jnp.moveaxis(a.reshape(a.shape[0], a.shape[1], nb, Q_BLOCK, a.shape[-1]), 2, 0)

    def body(args):
        ql, qp, qdb, qpos = args
        mask = qpos[:, None] >= pos[None, :]
        st1 = _sm_update(_sm_init(b, h, Q_BLOCK, KV_LORA), _mla_logits(ql, qp, ckv, kpe), ckv, mask)
        st2 = _sm_update(_sm_init(b, 2 * N_DIFF_HEADS, Q_BLOCK, DIFF_V),
                         _diff_logits(qdb, dk, _t5_bias(rel_bias, qpos, pos)), dv, mask)
        return _merge_groups(st1, st2, w_ukv, lam, lam_init, g_sub)

    out = lax.map(body, (blocks(q_lat), blocks(q_pe), blocks(qd), pos.reshape(nb, Q_BLOCK)))
    return jnp.moveaxis(out, 0, 1).reshape(b, s, MIX_WIDTH)


def _sample_mix(q, kv, pos, layer, cache_ckv, cache_kpe, cache_dk, cache_dv, page_table, rel_bias,
                w_ukv, lam, lam_init, g_sub):
    q_lat, q_pe, qd = q
    ckv, kpe, dk, dv = kv
    b, h, nq, _ = q_lat.shape
    offs = jnp.arange(PAGE_SIZE, dtype=jnp.int32)

    def step(carry, xs):
        st1, st2 = carry
        phys, page = xs
        pc = cache_ckv[layer, phys]
        st1 = _sm_update(st1, _mla_logits(q_lat, q_pe, pc, cache_kpe[layer, phys]), pc)
        bias = _t5_bias(rel_bias, pos, page * PAGE_SIZE + offs)
        st2 = _sm_update(st2, _diff_logits(qd, cache_dk[layer, phys], bias), cache_dv[layer, phys])
        return (st1, st2), None

    init = (_sm_init(b, h, nq, KV_LORA), _sm_init(b, 2 * N_DIFF_HEADS, nq, DIFF_V))
    n_pages = page_table.shape[1]
    (st1, st2), _ = lax.scan(step, init, (page_table.T, jnp.arange(n_pages, dtype=jnp.int32)))
    mask = pos[:, None] >= pos[None, :]
    st1 = _sm_update(st1, _mla_logits(q_lat, q_pe, ckv, kpe), ckv, mask)
    st2 = _sm_update(st2, _diff_logits(qd, dk, _t5_bias(rel_bias, pos, pos)), dv, mask)
    return _merge_groups(st1, st2, w_ukv, lam, lam_init, g_sub)


def _moe(x, layer, w_router, b_router, w_gate_up, b_gate_up, w_down, b_down):
    logits = (x @ w_router[layer]).astype(jnp.float32) + b_router[layer].astype(jnp.float32)
    top_v, top_i = lax.top_k(logits, TOP_K)
    gates = jax.nn.softmax(top_v, axis=-1)
    combine = jnp.einsum('tk,tke->te', gates, jax.nn.one_hot(top_i, N_EXPERTS, dtype=jnp.float32))
    y = jnp.zeros(x.shape, jnp.float32)
    for e in range(N_EXPERTS):
        hgu = x @ w_gate_up[layer, e] + b_gate_up[layer, e]
        gate = jnp.minimum(hgu[:, 0::2], SWIGLU_LIMIT)
        up = jnp.clip(hgu[:, 1::2], -SWIGLU_LIMIT, SWIGLU_LIMIT)
        act = (up + 1.0) * gate * jax.nn.sigmoid(SWIGLU_ALPHA * gate)
        y = y + combine[:, e:e + 1] * (act @ w_down[layer, e] + b_down[layer, e])
    return y


def _stack_rows(rows, i):
    return jnp.stack([r[i] for r in rows])


def setup_inputs(seed: int = 0) -> dict:
    key = jax.random.key(seed)
    ks = jax.random.split(key, 32)
    f32 = jnp.float32
    n_pages = PAST_LEN // PAGE_SIZE
    n_phys = (DEC_BATCH * n_pages * 5) // 4

    def nrm(k, shape, scale):
        return jax.random.normal(k, shape, f32) * scale

    in_scale = jnp.concatenate([jnp.ones((IN_WIDTH - DIFF_V,), f32), jnp.full((DIFF_V,), DN_BETA, f32)])
    ukv_scale = jnp.concatenate([jnp.ones((QK_NOPE,), f32), jnp.full((V_HEAD,), DN_BETA, f32)])
    page_table = jax.random.permutation(ks[6], n_phys)[: DEC_BATCH * n_pages].reshape(DEC_BATCH, n_pages).astype(jnp.int32)
    return {
        'x_prompt': nrm(ks[0], (BATCH, SEQ, D_MODEL), 1.0),
        'x_sample': nrm(ks[1], (DEC_BATCH, DEC_SEQ, D_MODEL), 1.0),
        'cache_ckv': nrm(ks[2], (DEPTH, n_phys, PAGE_SIZE, KV_LORA), 1.0),
        'cache_kpe': nrm(ks[3], (DEPTH, n_phys, PAGE_SIZE, QK_ROPE), 1.0),
        'cache_dk': nrm(ks[4], (DEPTH, n_phys, PAGE_SIZE, 2, DIFF_HEAD), 1.0),
        'cache_dv': nrm(ks[5], (DEPTH, n_phys, PAGE_SIZE, DIFF_V), DN_BETA),
        'page_table': page_table,
        'rel_bias': nrm(ks[7], (N_BUCKETS, N_DIFF_HEADS), 0.2),
        'w_in': nrm(ks[8], (DEPTH, D_MODEL, IN_WIDTH), D_MODEL ** -0.5) * in_scale,
        'g_q': 1.0 + nrm(ks[9], (DEPTH, Q_LORA), 0.01),
        'w_uq': nrm(ks[10], (DEPTH, Q_LORA, N_MLA_HEADS * (QK_NOPE + QK_ROPE)), Q_LORA ** -0.5),
        'g_kv': 1.0 + nrm(ks[11], (DEPTH, KV_LORA), 0.01),
        'w_ukv': nrm(ks[12], (DEPTH, KV_LORA, N_MLA_HEADS, QK_NOPE + V_HEAD), KV_LORA ** -0.5) * ukv_scale,
        'diff_lambda': nrm(ks[13], (DEPTH, 4, DIFF_HEAD), 0.1),
        'g_sub': 1.0 + nrm(ks[14], (DEPTH, DIFF_V), 0.01),
        'w_out': nrm(ks[15], (DEPTH, MIX_WIDTH, D_MODEL), MIX_WIDTH ** -0.5 * DN_BETA),
        'ln1_g': 1.0 + nrm(ks[16], (DEPTH, D_MODEL), 0.01),
        'ln1_b': nrm(ks[17], (DEPTH, D_MODEL), 0.01),
        'w_router': nrm(ks[18], (DEPTH, D_MODEL, N_EXPERTS), D_MODEL ** -0.5),
        'b_router': nrm(ks[19], (DEPTH, N_EXPERTS), 0.01),
        'w_gate_up': nrm(ks[20], (DEPTH, N_EXPERTS, D_MODEL, 2 * D_EXPERT), D_MODEL ** -0.5),
        'b_gate_up': nrm(ks[21], (DEPTH, N_EXPERTS, 2 * D_EXPERT), 0.01),
        'w_down': nrm(ks[22], (DEPTH, N_EXPERTS, D_EXPERT, D_MODEL), D_EXPERT ** -0.5 * DN_BETA),
        'b_down': nrm(ks[23], (DEPTH, N_EXPERTS, D_MODEL), 0.01),
        'ln2_g': 1.0 + nrm(ks[24], (DEPTH, D_MODEL), 0.01),
        'ln2_b': nrm(ks[25], (DEPTH, D_MODEL), 0.01),
    }


def reference(x_prompt, x_sample, cache_ckv, cache_kpe, cache_dk, cache_dv, page_table, rel_bias,
              w_in, g_q, w_uq, g_kv, w_ukv, diff_lambda, g_sub, w_out, ln1_g, ln1_b,
              w_router, b_router, w_gate_up, b_gate_up, w_down, b_down, ln2_g, ln2_b):
    bp, sp, _ = x_prompt.shape
    bs, ss, _ = x_sample.shape
    past_len = page_table.shape[1] * PAGE_SIZE
    pos_p = jnp.arange(sp, dtype=jnp.int32)
    pos_s = past_len + jnp.arange(ss, dtype=jnp.int32)
    xp, xs = x_prompt, x_sample
    new_p, new_s = [], []
    for l in range(DEPTH):
        lam_init = 0.8 - 0.6 * math.exp(-0.3 * l)
        dl = diff_lambda[l].astype(jnp.float32)
        lam = jnp.exp(jnp.sum(dl[0] * dl[1])) - jnp.exp(jnp.sum(dl[2] * dl[3])) + lam_init
        proj = (w_in[l], g_q[l], w_uq[l], g_kv[l], w_ukv[l])
        q_p, kv_p = _project(xp, pos_p, *proj)
        q_s, kv_s = _project(xs, pos_s, *proj)
        heads = (w_ukv[l], lam, lam_init, g_sub[l])
        mix_p = _prompt_mix(q_p, kv_p, pos_p, rel_bias, *heads)
        mix_s = _sample_mix(q_s, kv_s, pos_s, l, cache_ckv, cache_kpe, cache_dk, cache_dv,
                            page_table, rel_bias, *heads)
        resid = jnp.concatenate([xp.reshape(bp * sp, D_MODEL), xs.reshape(bs * ss, D_MODEL)], axis=0)
        mix = jnp.concatenate([mix_p.reshape(bp * sp, MIX_WIDTH), mix_s.reshape(bs * ss, MIX_WIDTH)], axis=0)
        hmid = _layernorm(DN_ALPHA * resid + mix @ w_out[l], ln1_g[l], ln1_b[l]).astype(x_prompt.dtype)
        ffn = _moe(hmid, l, w_router, b_router, w_gate_up, b_gate_up, w_down, b_down)
        hout = _layernorm(DN_ALPHA * hmid + ffn, ln2_g[l], ln2_b[l]).astype(x_prompt.dtype)
        xp = hout[: bp * sp].reshape(bp, sp, D_MODEL)
        xs = hout[bp * sp:].reshape(bs, ss, D_MODEL)
        new_p.append(kv_p)
        new_s.append(kv_s)
    return (xp, xs,
            _stack_rows(new_p, 0), _stack_rows(new_p, 1), _stack_rows(new_p, 2), _stack_rows(new_p, 3),
            _stack_rows(new_s, 0), _stack_rows(new_s, 1), _stack_rows(new_s, 2), _stack_rows(new_s, 3))
```

```python
import functools
import math

import jax
import jax.numpy as jnp
import numpy as np
from jax import lax
from jax.experimental import pallas as pl
from jax.experimental.pallas import tpu as pltpu

V_HEAD = 128
KV_LORA = 512
QK_NOPE = 128
QK_ROPE = 64
QK_ALL = KV_LORA + QK_ROPE
ROPE_THETA = 10000.0
MLA_SCALE = (QK_NOPE + QK_ROPE) ** -0.5
DIFF_HEAD = 128
DIFF_V = 2 * DIFF_HEAD
DIFF_SCALE = DIFF_HEAD ** -0.5
SUBLN_EPS = 1e-5
N_BUCKETS = 32
MAX_DISTANCE = 128
TOP_K = 4
SWIGLU_LIMIT = 7.0
SWIGLU_ALPHA = 1.702
LN_EPS = 1e-5
RMS_EPS = 1e-6

LANE = 128
VMEM_LIMIT = 60 * 1024 * 1024

ROW_SUB = 256
ROWS_PER_ITEM = 1280
FF_TILE = 256
DOWN_TILE = 512
PAGES_PER_STEP = 8

F32 = jnp.float32
BF16 = jnp.bfloat16


def _pick(n, cands):
    for c in cands:
        if n % c == 0:
            return c
    return n


def _cparams(sem, vmem=VMEM_LIMIT):
    return pltpu.CompilerParams(dimension_semantics=sem, vmem_limit_bytes=vmem)


def _mm_kernel(x_ref, w_ref, o_ref):
    o_ref[...] = jnp.dot(x_ref[...], w_ref[...], preferred_element_type=F32).astype(o_ref.dtype)


def _matmul(x, w, tm, tn, out_dtype, name):
    m, k = x.shape
    n = w.shape[1]
    return pl.pallas_call(
        _mm_kernel,
        grid=(m // tm, n // tn),
        in_specs=[pl.BlockSpec((tm, k), lambda i, j: (i, 0)),
                  pl.BlockSpec((k, tn), lambda i, j: (0, j))],
        out_specs=pl.BlockSpec((tm, tn), lambda i, j: (i, j)),
        out_shape=jax.ShapeDtypeStruct((m, n), out_dtype),
        compiler_params=_cparams(("parallel", "arbitrary")),
        name=name,
    )(x, w)


def _mm2_kernel(a_ref, b_ref, wa_ref, wb_ref, o_ref):
    acc = jnp.dot(a_ref[...], wa_ref[...], preferred_element_type=F32)
    acc += jnp.dot(b_ref[...], wb_ref[...], preferred_element_type=F32)
    o_ref[...] = acc


def _matmul2(a, b, wa, wb, tm, tn, name):
    m, ka = a.shape
    kb = b.shape[1]
    n = wa.shape[1]
    return pl.pallas_call(
        _mm2_kernel,
        grid=(m // tm, n // tn),
        in_specs=[pl.BlockSpec((tm, ka), lambda i, j: (i, 0)),
                  pl.BlockSpec((tm, kb), lambda i, j: (i, 0)),
                  pl.BlockSpec((ka, tn), lambda i, j: (0, j)),
                  pl.BlockSpec((kb, tn), lambda i, j: (0, j))],
        out_specs=pl.BlockSpec((tm, tn), lambda i, j: (i, j)),
        out_shape=jax.ShapeDtypeStruct((m, n), F32),
        compiler_params=_cparams(("parallel", "arbitrary")),
        name=name,
    )(a, b, wa, wb)


def _rope128(x, cos2, sin2):
    lane = lax.broadcasted_iota(jnp.int32, x.shape, 1)
    first_half = (lane & (QK_ROPE - 1)) < (QK_ROPE // 2)
    rot = jnp.where(first_half, pltpu.roll(x, LANE - QK_ROPE // 2, 1), pltpu.roll(x, QK_ROPE // 2, 1))
    return x * cos2 + rot * sin2


def _kv_post_kernel(hc_ref, hd_ref, hp_ref, g_ref, cos_ref, sin_ref,
                    ckv_ref, kpe_ref, dk_ref, dv_ref, kall_ref, dkv_ref):
    hc = hc_ref[...]
    ckv = hc * lax.rsqrt(jnp.mean(hc * hc, axis=-1, keepdims=True) + RMS_EPS) * g_ref[...]
    ckv_ref[...] = ckv
    kpe = _rope128(hp_ref[...], cos_ref[...], sin_ref[...])[:, :QK_ROPE]
    kpe_ref[...] = kpe
    kall_ref[:, :KV_LORA] = ckv.astype(BF16)
    kall_ref[:, KV_LORA:] = kpe.astype(BF16)
    hd = hd_ref[...]
    dk_ref[...] = hd[:, :DIFF_V]
    dv_ref[...] = hd[:, DIFF_V:]
    dkv_ref[...] = hd.astype(BF16)


def _kv_post(h2, g_kv, cos2, sin2, q_lora, pe_off, tm):
    t = h2.shape[0]
    row = lambda i: (i, 0)
    return pl.pallas_call(
        _kv_post_kernel,
        grid=(t // tm,),
        in_specs=[pl.BlockSpec((tm, KV_LORA), lambda i: (i, q_lora // KV_LORA)),
                  pl.BlockSpec((tm, 2 * DIFF_V), lambda i: (i, (q_lora + KV_LORA) // (2 * DIFF_V))),
                  pl.BlockSpec((tm, LANE), lambda i: (i, pe_off // LANE)),
                  pl.BlockSpec((1, KV_LORA), lambda i: (0, 0)),
                  pl.BlockSpec((tm, LANE), row),
                  pl.BlockSpec((tm, LANE), row)],
        out_specs=[pl.BlockSpec((tm, KV_LORA), row),
                   pl.BlockSpec((tm, QK_ROPE), row),
                   pl.BlockSpec((tm, DIFF_V), row),
                   pl.BlockSpec((tm, DIFF_V), row),
                   pl.BlockSpec((tm, QK_ALL), row),
                   pl.BlockSpec((tm, 2 * DIFF_V), row)],
        out_shape=[jax.ShapeDtypeStruct((t, KV_LORA), F32),
                   jax.ShapeDtypeStruct((t, QK_ROPE), F32),
                   jax.ShapeDtypeStruct((t, DIFF_V), F32),
                   jax.ShapeDtypeStruct((t, DIFF_V), F32),
                   jax.ShapeDtypeStruct((t, QK_ALL), BF16),
                   jax.ShapeDtypeStruct((t, 2 * DIFF_V), BF16)],
        compiler_params=_cparams(("parallel",)),
        name="kv_post",
    )(h2, h2, h2, g_kv, cos2, sin2)


def _q_proj_kernel(hq_ref, g_ref, wuq_ref, wk_ref, cos_ref, sin_ref, o_ref, *, n_heads):
    hq = hq_ref[...]
    qn = hq * lax.rsqrt(jnp.mean(hq * hq, axis=-1, keepdims=True) + RMS_EPS) * g_ref[...]
    q = jnp.dot(qn.astype(BF16), wuq_ref[...], preferred_element_type=F32)
    for h in range(n_heads):
        qh = q[:, h * QK_NOPE:(h + 1) * QK_NOPE].astype(BF16)
        o_ref[h, :, :KV_LORA] = jnp.dot(qh, wk_ref[h], preferred_element_type=F32).astype(BF16)
    pe0 = n_heads * QK_NOPE
    cos2, sin2 = cos_ref[...], sin_ref[...]
    for g in range(n_heads // 2):
        r = _rope128(q[:, pe0 + g * LANE: pe0 + (g + 1) * LANE], cos2, sin2).astype(BF16)
        o_ref[2 * g, :, KV_LORA:] = r[:, :QK_ROPE]
        o_ref[2 * g + 1, :, KV_LORA:] = r[:, QK_ROPE:]


def _q_proj(h2, g_q, wuq, wk, cos2, sin2, q_lora, n_heads, tm):
    t = h2.shape[0]
    return pl.pallas_call(
        functools.partial(_q_proj_kernel, n_heads=n_heads),
        grid=(t // tm,),
        in_specs=[pl.BlockSpec((tm, q_lora), lambda i: (i, 0)),
                  pl.BlockSpec((1, q_lora), lambda i: (0, 0)),
                  pl.BlockSpec(wuq.shape, lambda i: (0, 0)),
                  pl.BlockSpec(wk.shape, lambda i: (0, 0, 0)),
                  pl.BlockSpec((tm, LANE), lambda i: (i, 0)),
                  pl.BlockSpec((tm, LANE), lambda i: (i, 0))],
        out_specs=pl.BlockSpec((n_heads, tm, QK_ALL), lambda i: (0, i, 0)),
        out_shape=jax.ShapeDtypeStruct((n_heads, t, QK_ALL), BF16),
        compiler_params=_cparams(("parallel",)),
        name="q_proj",
    )(h2, g_q, wuq, wk, cos2, sin2)


def _sm_step(s, v_bf, m_ref, l_ref, acc_ref):
    m_old = m_ref[...]
    m_new = jnp.maximum(m_old, jnp.max(s, axis=-1, keepdims=True))
    corr = jnp.exp(m_old - m_new)
    p = jnp.exp(s - m_new)
    l_ref[...] = l_ref[...] * corr + jnp.sum(p, axis=-1, keepdims=True)
    acc_ref[...] = acc_ref[...] * corr + jnp.dot(p.astype(BF16), v_bf, preferred_element_type=F32)
    m_ref[...] = m_new


def _sm_reset(m_ref, l_ref, acc_ref):
    m_ref[...] = jnp.full(m_ref.shape, -jnp.inf, F32)
    l_ref[...] = jnp.zeros(l_ref.shape, F32)
    acc_ref[...] = jnp.zeros(acc_ref.shape, F32)


_NT = (((1,), (1,)), ((), ()))


def _mla_prompt_kernel(q_ref, k_ref, wv_ref, o_ref, m_ref, l_ref, acc_ref, *, blk):
    i = pl.program_id(1)
    q = q_ref[...]
    _sm_reset(m_ref, l_ref, acc_ref)

    def logits(j):
        k = k_ref[pl.ds(pl.multiple_of(j * blk, blk), blk), :]
        s = lax.dot_general(q, k, _NT, preferred_element_type=F32) * MLA_SCALE
        return s, k[:, :KV_LORA]

    def body(j, c):
        s, v = logits(j)
        _sm_step(s, v, m_ref, l_ref, acc_ref)
        return c

    lax.fori_loop(0, i, body, 0)
    s, v = logits(i)
    row = lax.broadcasted_iota(jnp.int32, s.shape, 0)
    col = lax.broadcasted_iota(jnp.int32, s.shape, 1)
    _sm_step(jnp.where(row >= col, s, -jnp.inf), v, m_ref, l_ref, acc_ref)
    o_lat = (acc_ref[...] / l_ref[...]).astype(BF16)
    o_ref[...] = jnp.dot(o_lat, wv_ref[...], preferred_element_type=F32).astype(o_ref.dtype)


def _mla_prompt(q_all, kall, wv, batch, seq, blk):
    n_heads = q_all.shape[0]
    nq = seq // blk
    return pl.pallas_call(
        functools.partial(_mla_prompt_kernel, blk=blk),
        grid=(batch, nq, n_heads),
        in_specs=[pl.BlockSpec((None, blk, QK_ALL), lambda b, i, h: (h, b * nq + i, 0)),
                  pl.BlockSpec((seq, QK_ALL), lambda b, i, h: (b, 0)),
                  pl.BlockSpec((None, KV_LORA, V_HEAD), lambda b, i, h: (h, 0, 0))],
        out_specs=pl.BlockSpec((blk, V_HEAD), lambda b, i, h: (b * nq + i, h)),
        out_shape=jax.ShapeDtypeStruct((batch * seq, n_heads * V_HEAD), BF16),
        scratch_shapes=[pltpu.VMEM((blk, 1), F32), pltpu.VMEM((blk, 1), F32),
                        pltpu.VMEM((blk, KV_LORA), F32)],
        compiler_params=_cparams(("parallel", "parallel", "arbitrary")),
        name="mla_prompt",
    )(q_all, kall, wv)


def _diff_lambda(dl, lam_init):
    s1 = jnp.sum(dl[0:1, :] * dl[1:2, :], axis=-1, keepdims=True)
    s2 = jnp.sum(dl[2:3, :] * dl[3:4, :], axis=-1, keepdims=True)
    return jnp.exp(s1) - jnp.exp(s2) + lam_init


def _diff_merge(o1, o2, lam, g_sub, lam_init):
    d = o1 - lam * o2
    d = d * lax.rsqrt(jnp.mean(d * d, axis=-1, keepdims=True) + SUBLN_EPS) * g_sub
    return d * (1.0 - lam_init)


def _diff_prompt_kernel(q1_ref, q2_ref, kv_ref, bias_ref, far_ref, dl_ref, g_ref, o_ref,
                        m1, l1, a1, m2, l2, a2, *, blk, lam_init):
    i = pl.program_id(1)
    q1 = q1_ref[...].astype(BF16)
    q2 = q2_ref[...].astype(BF16)
    _sm_reset(m1, l1, a1)
    _sm_reset(m2, l2, a2)

    def logits(j):
        kv = kv_ref[pl.ds(pl.multiple_of(j * blk, blk), blk), :]
        s1 = lax.dot_general(q1, kv[:, :DIFF_HEAD], _NT, preferred_element_type=F32) * DIFF_SCALE
        s2 = lax.dot_general(q2, kv[:, DIFF_HEAD:DIFF_V], _NT, preferred_element_type=F32) * DIFF_SCALE
        return s1, s2, kv[:, DIFF_V:]

    def far_body(j, c):
        s1, s2, v = logits(j)
        far = far_ref[0:1, 0:1]
        _sm_step(s1 + far, v, m1, l1, a1)
        _sm_step(s2 + far, v, m2, l2, a2)
        return c

    lax.fori_loop(0, i - 1, far_body, 0)

    @pl.when(i >= 1)
    def _():
        s1, s2, v = logits(i - 1)
        b = bias_ref[1]
        _sm_step(s1 + b, v, m1, l1, a1)
        _sm_step(s2 + b, v, m2, l2, a2)

    s1, s2, v = logits(i)
    row = lax.broadcasted_iota(jnp.int32, s1.shape, 0)
    col = lax.broadcasted_iota(jnp.int32, s1.shape, 1)
    keep = row >= col
    b = bias_ref[0]
    _sm_step(jnp.where(keep, s1 + b, -jnp.inf), v, m1, l1, a1)
    _sm_step(jnp.where(keep, s2 + b, -jnp.inf), v, m2, l2, a2)
    lam = _diff_lambda(dl_ref[...], lam_init)
    o = _diff_merge(a1[...] / l1[...], a2[...] / l2[...], lam, g_ref[...], lam_init)
    o_ref[...] = o.astype(o_ref.dtype)


def _diff_prompt(h2, dkv, bias_tiles, far_tiles, dl, g_sub, batch, seq, blk, qd_off, n_dh, lam_init):
    nq = seq // blk
    c0 = qd_off // DIFF_HEAD
    return pl.pallas_call(
        functools.partial(_diff_prompt_kernel, blk=blk, lam_init=lam_init),
        grid=(batch, nq, n_dh),
        in_specs=[pl.BlockSpec((blk, DIFF_HEAD), lambda b, i, h: (b * nq + i, c0 + h)),
                  pl.BlockSpec((blk, DIFF_HEAD), lambda b, i, h: (b * nq + i, c0 + n_dh + h)),
                  pl.BlockSpec((seq, 2 * DIFF_V), lambda b, i, h: (b, 0)),
                  pl.BlockSpec((None, 2, blk, blk), lambda b, i, h: (h, 0, 0, 0)),
                  pl.BlockSpec((None, 8, LANE), lambda b, i, h: (h, 0, 0)),
                  pl.BlockSpec((4, DIFF_HEAD), lambda b, i, h: (0, 0)),
                  pl.BlockSpec((1, DIFF_V), lambda b, i, h: (0, 0))],
        out_specs=pl.BlockSpec((blk, DIFF_V), lambda b, i, h: (b * nq + i, h)),
        out_shape=jax.ShapeDtypeStruct((batch * seq, n_dh * DIFF_V), BF16),
        scratch_shapes=[pltpu.VMEM((blk, 1), F32), pltpu.VMEM((blk, 1), F32), pltpu.VMEM((blk, DIFF_V), F32),
                        pltpu.VMEM((blk, 1), F32), pltpu.VMEM((blk, 1), F32), pltpu.VMEM((blk, DIFF_V), F32)],
        compiler_params=_cparams(("parallel", "parallel", "arbitrary")),
        name="diff_prompt",
    )(h2, h2, dkv, bias_tiles, far_tiles, dl, g_sub)


def _decode_kernel(pt_ref, qm_ref, qd_ref, ks_ref, dks_ref, bfar_ref, blast_ref, bself_ref,
                   ckv_hbm, kpe_hbm, dk_hbm, dv_hbm,
                   olat_ref, od_ref,
                   ckv_buf, kpe_buf, dk_buf, dv_buf, sems,
                   mm, lm, am, md, ld, ad, *, layer, n_groups, n_batch, gp, page, n_dh):
    b = pl.program_id(0)
    g = pl.program_id(1)
    step = b * n_groups + g
    slot = step % 2

    def copies(bb, gg, sl):
        out = []
        for j in range(gp):
            phys = pt_ref[bb, gg * gp + j]
            out.append(pltpu.make_async_copy(ckv_hbm.at[layer, phys], ckv_buf.at[sl, j], sems.at[sl, 0]))
            out.append(pltpu.make_async_copy(kpe_hbm.at[layer, phys], kpe_buf.at[sl, j], sems.at[sl, 1]))
            out.append(pltpu.make_async_copy(dk_hbm.at[layer, phys, :, 0, :],
                                             dk_buf.at[sl, j, :, pl.ds(0, DIFF_HEAD)], sems.at[sl, 2]))
            out.append(pltpu.make_async_copy(dk_hbm.at[layer, phys, :, 1, :],
                                             dk_buf.at[sl, j, :, pl.ds(DIFF_HEAD, DIFF_HEAD)], sems.at[sl, 2]))
            out.append(pltpu.make_async_copy(dv_hbm.at[layer, phys], dv_buf.at[sl, j], sems.at[sl, 3]))
        return out

    @pl.when(step == 0)
    def _():
        for c in copies(0, 0, 0):
            c.start()

    nxt = step + 1

    @pl.when(nxt < n_batch * n_groups)
    def _():
        for c in copies(nxt // n_groups, nxt % n_groups, nxt % 2):
            c.start()

    @pl.when(g == 0)
    def _():
        _sm_reset(mm, lm, am)
        _sm_reset(md, ld, ad)

    for c in copies(b, g, slot):
        c.wait()

    nk = gp * page
    qm = qm_ref[...]
    qd = qd_ref[...]
    kc = ckv_buf[slot].reshape(nk, KV_LORA).astype(BF16)
    kp = kpe_buf[slot].reshape(nk, QK_ROPE).astype(BF16)
    s = lax.dot_general(qm[:, :KV_LORA], kc, _NT, preferred_element_type=F32)
    s += lax.dot_general(qm[:, KV_LORA:], kp, _NT, preferred_element_type=F32)
    _sm_step(s * MLA_SCALE, kc, mm, lm, am)

    dk = dk_buf[slot].reshape(nk, DIFF_V).astype(BF16)
    dv = dv_buf[slot].reshape(nk, DIFF_V).astype(BF16)
    s1 = lax.dot_general(qd[:n_dh], dk[:, :DIFF_HEAD], _NT, preferred_element_type=F32)
    s2 = lax.dot_general(qd[n_dh:], dk[:, DIFF_HEAD:], _NT, preferred_element_type=F32)
    bias = jnp.where(g == n_groups - 1, blast_ref[...], bfar_ref[...])
    sd = jnp.concatenate([s1, s2], axis=0) * DIFF_SCALE + bias
    _sm_step(sd, dv, md, ld, ad)

    @pl.when(g == n_groups - 1)
    def _():
        ks = ks_ref[...].astype(F32)
        s_self = jnp.sum(qm.astype(F32) * ks, axis=-1, keepdims=True) * MLA_SCALE
        m_old = mm[...]
        m_new = jnp.maximum(m_old, s_self)
        corr = jnp.exp(m_old - m_new)
        p = jnp.exp(s_self - m_new)
        l_new = lm[...] * corr + p
        olat_ref[...] = (am[...] * corr + p * ks[:, :KV_LORA]) / l_new

        dks = dks_ref[...].astype(F32)
        qdf = qd.astype(F32)
        t1 = jnp.sum(qdf[:n_dh] * dks[:, :DIFF_HEAD], axis=-1, keepdims=True)
        t2 = jnp.sum(qdf[n_dh:] * dks[:, DIFF_HEAD:DIFF_V], axis=-1, keepdims=True)
        sd_self = jnp.concatenate([t1, t2], axis=0) * DIFF_SCALE + bself_ref[:, 0:1]
        m_old = md[...]
        m_new = jnp.maximum(m_old, sd_self)
        corr = jnp.exp(m_old - m_new)
        p = jnp.exp(sd_self - m_new)
        l_new = ld[...] * corr + p
        od_ref[...] = (ad[...] * corr + p * dks[:, DIFF_V:]) / l_new


def _decode(page_table, qm, qd, kall_s, dkv_s, bfar, blast, bself,
            cache_ckv, cache_kpe, cache_dk, cache_dv, layer, gp):
    n_batch, n_heads, _ = qm.shape
    n_maps = qd.shape[1]
    n_pages = page_table.shape[1]
    page = cache_ckv.shape[2]
    n_groups = n_pages // gp
    nk = gp * page
    per_b = lambda b, g, pt: (b, 0, 0)
    const2 = lambda b, g, pt: (0, 0)
    any_spec = pl.BlockSpec(memory_space=pl.ANY)
    grid_spec = pltpu.PrefetchScalarGridSpec(
        num_scalar_prefetch=1,
        grid=(n_batch, n_groups),
        in_specs=[pl.BlockSpec((None, n_heads, QK_ALL), per_b),
                  pl.BlockSpec((None, n_maps, DIFF_HEAD), per_b),
                  pl.BlockSpec((None, 1, QK_ALL), per_b),
                  pl.BlockSpec((None, 1, 2 * DIFF_V), per_b),
                  pl.BlockSpec((n_maps, nk), const2),
                  pl.BlockSpec((n_maps, nk), const2),
                  pl.BlockSpec((n_maps, LANE), const2),
                  any_spec, any_spec, any_spec, any_spec],
        out_specs=[pl.BlockSpec((None, n_heads, KV_LORA), per_b),
                   pl.BlockSpec((None, n_maps, DIFF_V), per_b)],
        scratch_shapes=[pltpu.VMEM((2, gp, page, KV_LORA), F32),
                        pltpu.VMEM((2, gp, page, QK_ROPE), F32),
                        pltpu.VMEM((2, gp, page, DIFF_V), F32),
                        pltpu.VMEM((2, gp, page, DIFF_V), F32),
                        pltpu.SemaphoreType.DMA((2, 4)),
                        pltpu.VMEM((n_heads, 1), F32), pltpu.VMEM((n_heads, 1), F32),
                        pltpu.VMEM((n_heads, KV_LORA), F32),
                        pltpu.VMEM((n_maps, 1), F32), pltpu.VMEM((n_maps, 1), F32),
                        pltpu.VMEM((n_maps, DIFF_V), F32)])
    return pl.pallas_call(
        functools.partial(_decode_kernel, layer=layer, n_groups=n_groups, n_batch=n_batch, gp=gp,
                          page=page, n_dh=n_maps // 2),
        grid_spec=grid_spec,
        out_shape=[jax.ShapeDtypeStruct((n_batch, n_heads, KV_LORA), F32),
                   jax.ShapeDtypeStruct((n_batch, n_maps, DIFF_V), F32)],
        compiler_params=_cparams(("arbitrary", "arbitrary")),
        name="decode_attn",
    )(page_table, qm, qd, kall_s, dkv_s, bfar, blast, bself, cache_ckv, cache_kpe, cache_dk, cache_dv)


def _decode_merge_kernel(olat_ref, wv_ref, od_ref, dl_ref, g_ref, omla_ref, odiff_ref, *, n_dh, lam_init):
    h = pl.program_id(0)
    omla_ref[...] = jnp.dot(olat_ref[...].astype(BF16), wv_ref[...],
                            preferred_element_type=F32).astype(omla_ref.dtype)

    @pl.when(h < n_dh)
    def _():
        lam = _diff_lambda(dl_ref[...], lam_init)
        odiff_ref[...] = _diff_merge(od_ref[0], od_ref[1], lam, g_ref[...], lam_init).astype(odiff_ref.dtype)


def _decode_merge(olat_h, wv, od_h, dl, g_sub, lam_init):
    n_heads, n_batch, _ = olat_h.shape
    n_dh = od_h.shape[1]
    return pl.pallas_call(
        functools.partial(_decode_merge_kernel, n_dh=n_dh, lam_init=lam_init),
        grid=(n_heads,),
        in_specs=[pl.BlockSpec((None, n_batch, KV_LORA), lambda h: (h, 0, 0)),
                  pl.BlockSpec((None, KV_LORA, V_HEAD), lambda h: (h, 0, 0)),
                  pl.BlockSpec((2, None, n_batch, DIFF_V), lambda h: (0, jnp.minimum(h, n_dh - 1), 0, 0)),
                  pl.BlockSpec((4, DIFF_HEAD), lambda h: (0, 0)),
                  pl.BlockSpec((1, DIFF_V), lambda h: (0, 0))],
        out_specs=[pl.BlockSpec((n_batch, V_HEAD), lambda h: (0, h)),
                   pl.BlockSpec((n_batch, DIFF_V), lambda h: (0, jnp.minimum(h, n_dh - 1)))],
        out_shape=[jax.ShapeDtypeStruct((n_batch, n_heads * V_HEAD), BF16),
                   jax.ShapeDtypeStruct((n_batch, n_dh * DIFF_V), BF16)],
        compiler_params=_cparams(("arbitrary",)),
        name="decode_merge",
    )(olat_h, wv, od_h, dl, g_sub)


def _layernorm(z, g, b):
    mu = jnp.mean(z, axis=-1, keepdims=True)
    zc = z - mu
    var = jnp.mean(zc * zc, axis=-1, keepdims=True)
    return zc * lax.rsqrt(var + LN_EPS) * g + b


def _ln_router_kernel(z_ref, x_ref, g_ref, b_ref, wr_ref, br_ref, h_ref, hbf_ref, ids_ref, gates_ref,
                      *, alpha, n_experts):
    hmid = _layernorm(alpha * x_ref[...] + z_ref[...], g_ref[...], b_ref[...])
    h_ref[...] = hmid
    hbf = hmid.astype(BF16)
    hbf_ref[...] = hbf
    logits = jnp.dot(hbf, wr_ref[...], preferred_element_type=F32) + br_ref[...]
    lane = lax.broadcasted_iota(jnp.int32, logits.shape, 1)
    out_lane = lax.broadcasted_iota(jnp.int32, ids_ref.shape, 1)
    ids = jnp.zeros(ids_ref.shape, jnp.int32)
    vals = jnp.zeros(gates_ref.shape, F32)
    top = None
    denom = None
    for k in range(TOP_K):
        v = jnp.max(logits, axis=-1, keepdims=True)
        idx = jnp.min(jnp.where(logits == v, lane, n_experts), axis=-1, keepdims=True)
        logits = jnp.where(lane == idx, -jnp.inf, logits)
        if k == 0:
            top = v
        e = jnp.exp(v - top)
        denom = e if k == 0 else denom + e
        ids = jnp.where(out_lane == k, idx, ids)
        vals = jnp.where(out_lane == k, e, vals)
    ids_ref[...] = ids
    gates_ref[...] = vals / denom


def _ln_router(z, x, g, b, wr, br, alpha, tm):
    t, d = z.shape
    n_experts = wr.shape[1]
    row = lambda i: (i, 0)
    c = lambda i: (0, 0)
    return pl.pallas_call(
        functools.partial(_ln_router_kernel, alpha=alpha, n_experts=n_experts),
        grid=(t // tm,),
        in_specs=[pl.BlockSpec((tm, d), row), pl.BlockSpec((tm, d), row),
                  pl.BlockSpec((1, d), c), pl.BlockSpec((1, d), c),
                  pl.BlockSpec((d, n_experts), c), pl.BlockSpec((1, n_experts), c)],
        out_specs=[pl.BlockSpec((tm, d), row), pl.BlockSpec((tm, d), row),
                   pl.BlockSpec((tm, LANE), row), pl.BlockSpec((tm, LANE), row)],
        out_shape=[jax.ShapeDtypeStruct((t, d), F32), jax.ShapeDtypeStruct((t, d), BF16),
                   jax.ShapeDtypeStruct((t, LANE), jnp.int32), jax.ShapeDtypeStruct((t, LANE), F32)],
        compiler_params=_cparams(("parallel",)),
        name="ln_router",
    )(z, x, g, b, wr, br)


def _moe_kernel(ex_ref, r0_ref, ns_ref,
                xs_hbm, perm_ref, wgu_ref, bgu_ref, wd_ref, bd_ref, ys_hbm,
                xbuf, act, wbf, ybuf, xsem, ysem, *, n_ff, n_items, d_model):
    w = pl.program_id(0)
    s = pl.program_id(1)
    nsub = ns_ref[w]
    row0 = r0_ref[w]

    def x_copy(item_row0, r):
        src = xs_hbm.at[pl.ds(pl.multiple_of(item_row0 + r * ROW_SUB, ROW_SUB), ROW_SUB)]
        return pltpu.make_async_copy(src, xbuf.at[pl.ds(r * ROW_SUB, ROW_SUB)], xsem.at[0])

    def fetch_rows(item):
        n = ns_ref[item]
        base = r0_ref[item]

        def go(r, c):
            x_copy(base, r).start()
            return c
        lax.fori_loop(0, n, go, 0)

    @pl.when((w == 0) & (s == 0))
    def _():
        fetch_rows(0)

    @pl.when(s == 0)
    def _():
        def go(r, c):
            x_copy(row0, r).wait()
            return c
        lax.fori_loop(0, nsub, go, 0)

    @pl.when(s < n_ff)
    def _():
        wbf[...] = wgu_ref[...].astype(BF16)
        bias = bgu_ref[...]
        perm = perm_ref[...]
        lane = lax.broadcasted_iota(jnp.int32, (ROW_SUB, LANE), 1)
        even = (lane & 1) == 0

        def sub(r, c):
            rows = pl.ds(pl.multiple_of(r * ROW_SUB, ROW_SUB), ROW_SUB)
            hgu = jnp.dot(xbuf[rows, :], wbf[...], preferred_element_type=F32) + bias
            for q in range(FF_TILE // LANE):
                a = hgu[:, (2 * q) * LANE:(2 * q + 1) * LANE]
                bq = hgu[:, (2 * q + 1) * LANE:(2 * q + 2) * LANE]
                gate = jnp.where(even, a, pltpu.roll(bq, 1, 1))
                up = jnp.where(even, pltpu.roll(a, LANE - 1, 1), bq)
                gate = jnp.minimum(gate, SWIGLU_LIMIT)
                up = jnp.clip(up, -SWIGLU_LIMIT, SWIGLU_LIMIT)
                mixed = ((up + 1.0) * gate * jax.nn.sigmoid(SWIGLU_ALPHA * gate)).astype(BF16)
                act[s, rows, q * LANE:(q + 1) * LANE] = jnp.dot(
                    mixed, perm, preferred_element_type=F32).astype(BF16)
            return c
        lax.fori_loop(0, nsub, sub, 0)

    @pl.when(s == n_ff)
    def _():
        @pl.when(w + 1 < n_items)
        def _():
            fetch_rows(w + 1)

    @pl.when(s >= n_ff)
    def _():
        n = s - n_ff
        wd_bf = wbf.at[pl.ds(0, n_ff * FF_TILE)]
        wd_bf[...] = wd_ref[...].astype(BF16)
        bias = bd_ref[...]

        def y_copy(r, sl):
            dst = ys_hbm.at[pl.ds(pl.multiple_of(row0 + r * ROW_SUB, ROW_SUB), ROW_SUB),
                            pl.ds(pl.multiple_of(n * DOWN_TILE, DOWN_TILE), DOWN_TILE)]
            return pltpu.make_async_copy(ybuf.at[sl], dst, ysem.at[sl])

        def sub(r, c):
            rows = pl.ds(pl.multiple_of(r * ROW_SUB, ROW_SUB), ROW_SUB)
            sl = r % 2

            @pl.when(r >= 2)
            def _():
                y_copy(r - 2, sl).wait()
            y = bias
            for j in range(n_ff):
                y = y + jnp.dot(act[j, rows, :], wd_bf[pl.ds(j * FF_TILE, FF_TILE), :],
                                preferred_element_type=F32)
            ybuf[sl] = y
            y_copy(r, sl).start()
            return c
        lax.fori_loop(0, nsub, sub, 0)

        @pl.when(nsub >= 2)
        def _():
            y_copy(nsub - 2, nsub % 2).wait()

        @pl.when(nsub >= 1)
        def _():
            y_copy(nsub - 1, (nsub - 1) % 2).wait()


def _moe_ffn(item_expert, item_row0, item_nsub, xs, perm, w_gate_up, b_gate_up, w_down, b_down, layer):
    n_rows, d_model = xs.shape
    n_experts, _, two_de = w_gate_up.shape[1:]
    d_expert = two_de // 2
    n_ff = d_expert // FF_TILE
    n_down = d_model // DOWN_TILE
    n_items = item_expert.shape[0]
    bgu = b_gate_up.reshape(b_gate_up.shape[0], n_experts, 1, two_de)
    bd = b_down.reshape(b_down.shape[0], n_experts, 1, d_model)

    def wgu_map(w, s, ex, r0, ns):
        return (layer, ex[w], 0, jnp.where(ns[w] > 0, jnp.minimum(s, n_ff - 1), n_ff - 1))

    def wd_map(w, s, ex, r0, ns):
        return (layer, ex[w], 0, jnp.where(ns[w] > 0, jnp.maximum(s - n_ff, 0), n_down - 1))

    grid_spec = pltpu.PrefetchScalarGridSpec(
        num_scalar_prefetch=3,
        grid=(n_items, n_ff + n_down),
        in_specs=[pl.BlockSpec(memory_space=pl.ANY),
                  pl.BlockSpec((LANE, LANE), lambda w, s, ex, r0, ns: (0, 0)),
                  pl.BlockSpec((None, None, d_model, 2 * FF_TILE), wgu_map),
                  pl.BlockSpec((None, None, 1, 2 * FF_TILE), wgu_map),
                  pl.BlockSpec((None, None, d_expert, DOWN_TILE), wd_map),
                  pl.BlockSpec((None, None, 1, DOWN_TILE), wd_map)],
        out_specs=pl.BlockSpec(memory_space=pl.ANY),
        scratch_shapes=[pltpu.VMEM((ROWS_PER_ITEM, d_model), BF16),
                        pltpu.VMEM((n_ff, ROWS_PER_ITEM, FF_TILE), BF16),
                        pltpu.VMEM((max(d_model, d_expert), 2 * FF_TILE), BF16),
                        pltpu.VMEM((2, ROW_SUB, DOWN_TILE), F32),
                        pltpu.SemaphoreType.DMA((1,)),
                        pltpu.SemaphoreType.DMA((2,))])
    return pl.pallas_call(
        functools.partial(_moe_kernel, n_ff=n_ff, n_items=n_items, d_model=d_model),
        grid_spec=grid_spec,
        out_shape=jax.ShapeDtypeStruct((n_rows, d_model), F32),
        compiler_params=_cparams(("arbitrary", "arbitrary")),
        name="moe_ffn",
    )(item_expert, item_row0, item_nsub, xs, perm, w_gate_up, bgu, w_down, bd)


def _combine_ln_kernel(y_ref, gates_ref, h_ref, g_ref, b_ref, o_ref, *, alpha):
    gates = gates_ref[...]
    ffn = gates[:, 0:1] * y_ref[0]
    for k in range(1, TOP_K):
        ffn = ffn + gates[:, k:k + 1] * y_ref[k]
    o_ref[...] = _layernorm(alpha * h_ref[...] + ffn, g_ref[...], b_ref[...])


def _combine_ln(yk, gates, hmid, g, b, alpha, tm):
    t, d = hmid.shape
    row = lambda i: (i, 0)
    c = lambda i: (0, 0)
    return pl.pallas_call(
        functools.partial(_combine_ln_kernel, alpha=alpha),
        grid=(t // tm,),
        in_specs=[pl.BlockSpec((TOP_K, tm, d), lambda i: (0, i, 0)),
                  pl.BlockSpec((tm, LANE), row), pl.BlockSpec((tm, d), row),
                  pl.BlockSpec((1, d), c), pl.BlockSpec((1, d), c)],
        out_specs=pl.BlockSpec((tm, d), row),
        out_shape=jax.ShapeDtypeStruct((t, d), F32),
        compiler_params=_cparams(("parallel",)),
        name="combine_ln",
    )(yk, gates, hmid, g, b)


def _rope_tables(pos):
    inv = ROPE_THETA ** (-jnp.arange(0, QK_ROPE, 2, dtype=F32) / QK_ROPE)
    ang = pos.astype(F32)[:, None] * inv[None, :]
    cos, sin = jnp.cos(ang), jnp.sin(ang)
    return jnp.tile(cos, (1, 4)), jnp.tile(jnp.concatenate([-sin, sin], axis=-1), (1, 2))


def _t5_bucket(dist):
    n = jnp.maximum(dist, 0)
    max_exact = N_BUCKETS // 2
    large = max_exact + (jnp.log(jnp.maximum(n, max_exact).astype(F32) / max_exact)
                         / math.log(MAX_DISTANCE / max_exact) * (N_BUCKETS - max_exact)).astype(jnp.int32)
    return jnp.where(n < max_exact, n, jnp.minimum(large, N_BUCKETS - 1))


def _route_layout(ids, n_experts, n_items, n_rows):
    t = ids.shape[0]
    flat = ids.reshape(-1)
    onehot = (flat[:, None] == jnp.arange(n_experts, dtype=jnp.int32)[None, :]).astype(jnp.int32)
    csum = jnp.cumsum(onehot, axis=0)
    counts = csum[-1]
    rank = jnp.take_along_axis(csum, flat[:, None], axis=1)[:, 0] - 1
    nsub_e = (counts + ROW_SUB - 1) // ROW_SUB
    seg_start = (jnp.cumsum(nsub_e) - nsub_e) * ROW_SUB
    slot = seg_start[flat] + rank
    tok_of_row = jnp.zeros((n_rows,), jnp.int32).at[slot].set(
        jnp.arange(t * TOP_K, dtype=jnp.int32) // TOP_K, unique_indices=True)
    sub_per_item = ROWS_PER_ITEM // ROW_SUB
    items_e = (nsub_e + sub_per_item - 1) // sub_per_item
    item_end = jnp.cumsum(items_e)
    item_start = item_end - items_e
    w = jnp.arange(n_items, dtype=jnp.int32)
    n_valid = item_end[-1]
    e_of_w = jnp.minimum(jnp.searchsorted(item_end, w, side="right"), n_experts - 1).astype(jnp.int32)
    chunk = w - item_start[e_of_w]
    nsub_w = jnp.clip(nsub_e[e_of_w] - chunk * sub_per_item, 0, sub_per_item)
    valid = w < n_valid
    last_e = e_of_w[jnp.maximum(n_valid - 1, 0)]
    item_expert = jnp.where(valid, e_of_w, last_e).astype(jnp.int32)
    item_nsub = jnp.where(valid, nsub_w, 0).astype(jnp.int32)
    item_row0 = jnp.where(valid, seg_start[e_of_w] + chunk * ROWS_PER_ITEM, 0).astype(jnp.int32)
    return slot.reshape(t, TOP_K), tok_of_row, item_expert, item_row0, item_nsub


def _act_perm():
    n = np.arange(LANE)
    src = np.where(n < LANE // 2, 2 * n, 2 * (n - LANE // 2) + 1)
    p = np.zeros((LANE, LANE), np.float32)
    p[src, n] = 1.0
    return jnp.asarray(p, BF16)


def kernel(x_prompt, x_sample, cache_ckv, cache_kpe, cache_dk, cache_dv, page_table, rel_bias,
           w_in, g_q, w_uq, g_kv, w_ukv, diff_lambda, g_sub, w_out, ln1_g, ln1_b,
           w_router, b_router, w_gate_up, b_gate_up, w_down, b_down, ln2_g, ln2_b):
    bp, sp, d = x_prompt.shape
    bs, ss, _ = x_sample.shape
    assert ss == 1, "decode path handles one new token per sequence"
    depth = w_in.shape[0]
    n_heads = (d // 2) // V_HEAD
    n_dh = (d // 2) // DIFF_V
    q_lora = w_uq.shape[1]
    n_experts = w_router.shape[2]
    page = cache_ckv.shape[2]
    n_pages = page_table.shape[1]
    past_len = n_pages * page
    tp, ts = bp * sp, bs * ss
    t = tp + ts
    dn_alpha = (2 * depth) ** 0.25

    c_ckv = q_lora
    c_dkv = c_ckv + KV_LORA
    c_qd = c_dkv + 2 * DIFF_V
    c_pe = c_qd + 2 * n_dh * DIFF_HEAD
    n_in = c_pe + LANE
    assert c_ckv % KV_LORA == 0 and c_dkv % (2 * DIFF_V) == 0
    assert (t * TOP_K) % ROW_SUB == 0
    i1 = q_lora
    i2 = i1 + KV_LORA + QK_ROPE
    i3 = i2 + 2 * n_dh * DIFF_HEAD
    i4 = i3 + 2 * DIFF_HEAD

    tm_tok = _pick(t, (640, 320, 256, 128, 64, 32, 16, 8))
    tm_big = _pick(t, (1664, 640, 320, 256, 128, 64, 32, 16, 8))
    blk = _pick(sp, (256, 128))
    gp = _pick(n_pages, (PAGES_PER_STEP, 4, 2, 1))
    assert sp % blk == 0 and blk >= MAX_DISTANCE and page >= MAX_DISTANCE

    pos_p = jnp.arange(sp, dtype=jnp.int32)
    pos_all = jnp.concatenate([jnp.tile(pos_p, bp), jnp.full((ts,), past_len, jnp.int32)])
    cos2, sin2 = _rope_tables(pos_all)

    r = jnp.arange(blk, dtype=jnp.int32)
    dist = r[:, None] - r[None, :]
    tile_idx = jnp.stack([_t5_bucket(dist), _t5_bucket(dist + blk)])
    far_idx = _t5_bucket(jnp.asarray(2 * blk, jnp.int32))
    offs = jnp.arange(page, dtype=jnp.int32)
    last_idx = _t5_bucket(past_len - ((n_pages - 1) * page + offs))
    far_dec_idx = _t5_bucket(jnp.asarray(past_len - ((n_pages - 1) * page - 1), jnp.int32))

    x_all = jnp.concatenate([x_prompt.reshape(tp, d), x_sample.reshape(ts, d)], axis=0)
    perm = _act_perm()
    new_p, new_s = [], []
    for l in range(depth):
        lam_init = 0.8 - 0.6 * math.exp(-0.3 * l)
        wi = w_in[l]
        w_in2 = jnp.concatenate(
            [wi[:, :i1], wi[:, i1:i1 + KV_LORA], wi[:, i3:i4], wi[:, i4:], wi[:, i2:i3],
             wi[:, i1 + KV_LORA:i2], jnp.zeros((d, LANE - QK_ROPE), wi.dtype)], axis=1).astype(BF16)
        wuq = w_uq[l].reshape(q_lora, n_heads, QK_NOPE + QK_ROPE)
        wuq2 = jnp.concatenate([wuq[:, :, :QK_NOPE].reshape(q_lora, -1),
                                wuq[:, :, QK_NOPE:].reshape(q_lora, -1)], axis=1).astype(BF16)
        wk = jnp.transpose(w_ukv[l][:, :, :QK_NOPE], (1, 2, 0)).astype(BF16)
        wv = jnp.transpose(w_ukv[l][:, :, QK_NOPE:], (1, 0, 2)).astype(BF16)
        wo = w_out[l].astype(BF16)
        half = n_heads * V_HEAD

        x_bf = x_all.astype(BF16)
        h2 = _matmul(x_bf, w_in2, tm_big, _pick(n_in, (384, 256, 128)), F32, "in_proj")
        ckv, kpe, dk, dv, kall, dkv = _kv_post(h2, g_kv[l][None, :], cos2, sin2, q_lora, c_pe, tm_tok)
        q_all = _q_proj(h2, g_q[l][None, :], wuq2, wk, cos2, sin2, q_lora, n_heads,
                        _pick(t, (320, 256, 128, 64, 32, 16, 8)))

        rb = rel_bias.astype(F32)
        bias_tiles = jnp.moveaxis(rb[tile_idx], -1, 0)
        far_tiles = jnp.broadcast_to(rb[far_idx][:, None, None], (n_dh, 8, LANE))
        dl = diff_lambda[l].astype(F32)
        gs = g_sub[l][None, :].astype(F32)
        mix_mla_p = _mla_prompt(q_all, kall, wv, bp, sp, blk)
        mix_diff_p = _diff_prompt(h2, dkv, bias_tiles, far_tiles, dl, gs, bp, sp, blk, c_qd, n_dh, lam_init)

        qm_s = jnp.transpose(q_all[:, tp:, :], (1, 0, 2))
        qd_s = h2[tp:, c_qd:c_qd + 2 * n_dh * DIFF_HEAD].astype(BF16).reshape(ts, 2 * n_dh, DIFF_HEAD)
        nk = gp * page
        bfar = jnp.broadcast_to(jnp.tile(rb[far_dec_idx], 2)[:, None], (2 * n_dh, nk))
        blast = jnp.concatenate([bfar[:, :nk - page], jnp.tile(rb[last_idx].T, (2, 1))], axis=1)
        bself = jnp.broadcast_to(jnp.tile(rb[0], 2)[:, None], (2 * n_dh, LANE))
        olat, od = _decode(page_table, qm_s, qd_s, kall[tp:, None, :], dkv[tp:, None, :],
                           bfar, blast, bself, cache_ckv, cache_kpe, cache_dk, cache_dv, l, gp)
        od_h = jnp.transpose(od.reshape(ts, 2, n_dh, DIFF_V), (1, 2, 0, 3))
        mix_mla_s, mix_diff_s = _decode_merge(jnp.transpose(olat, (1, 0, 2)), wv, od_h, dl, gs, lam_init)

        mix_a = jnp.concatenate([mix_mla_p, mix_mla_s], axis=0)
        mix_b = jnp.concatenate([mix_diff_p, mix_diff_s], axis=0)
        z = _matmul2(mix_a, mix_b, wo[:half], wo[half:], tm_big, _pick(d, (512, 256, 128)), "out_proj")
        hmid, hmid_bf, ids128, gates128 = _ln_router(
            z, x_all, ln1_g[l][None, :], ln1_b[l][None, :], w_router[l].astype(BF16),
            b_router[l][None, :].astype(F32), dn_alpha, _pick(t, (320, 256, 128, 64, 32, 16, 8)))

        n_rows = t * TOP_K + n_experts * ROW_SUB
        n_items = n_experts + (n_rows + ROWS_PER_ITEM - 1) // ROWS_PER_ITEM
        slot, tok_of_row, item_expert, item_row0, item_nsub = _route_layout(
            ids128[:, :TOP_K], n_experts, n_items, n_rows)
        xs = jnp.take(hmid_bf, tok_of_row, axis=0)
        ys = _moe_ffn(item_expert, item_row0, item_nsub, xs, perm, w_gate_up, b_gate_up, w_down, b_down, l)
        yk = jnp.take(ys, slot.T, axis=0)
        hout = _combine_ln(yk, gates128, hmid, ln2_g[l][None, :], ln2_b[l][None, :], dn_alpha,
                           _pick(t, (128, 64, 32, 16, 8)))
        x_all = hout
        new_p.append((ckv[:tp].reshape(bp, sp, KV_LORA), kpe[:tp].reshape(bp, sp, QK_ROPE),
                      dk[:tp].reshape(bp, sp, 2, DIFF_HEAD), dv[:tp].reshape(bp, sp, DIFF_V)))
        new_s.append((ckv[tp:].reshape(bs, ss, KV_LORA), kpe[tp:].reshape(bs, ss, QK_ROPE),
                      dk[tp:].reshape(bs, ss, 2, DIFF_HEAD), dv[tp:].reshape(bs, ss, DIFF_V)))

    stack = lambda rows, i: jnp.stack([r_[i] for r_ in rows])
    return (x_all[:tp].reshape(bp, sp, d), x_all[tp:].reshape(bs, ss, d),
            stack(new_p, 0), stack(new_p, 1), stack(new_p, 2), stack(new_p, 3),
            stack(new_s, 0), stack(new_s, 1), stack(new_s, 2), stack(new_s, 3))
```

```python
import functools
import math

import jax
import jax.numpy as jnp
import numpy as np
from jax import lax
from jax.experimental import pallas as pl
from jax.experimental.pallas import tpu as pltpu

V_HEAD = 128
KV_LORA = 512
QK_NOPE = 128
QK_ROPE = 64
QK_ALL = KV_LORA + QK_ROPE
ROPE_THETA = 10000.0
MLA_SCALE = (QK_NOPE + QK_ROPE) ** -0.5
DIFF_HEAD = 128
DIFF_V = 2 * DIFF_HEAD
DIFF_SCALE = DIFF_HEAD ** -0.5
DKV_WIDTH = 2 * DIFF_V + 128
SUBLN_EPS = 1e-5
N_BUCKETS = 32
MAX_DISTANCE = 128
TOP_K = 4
SWIGLU_LIMIT = 7.0
SWIGLU_ALPHA = 1.702
LN_EPS = 1e-5
RMS_EPS = 1e-6

LANE = 128
VMEM_LIMIT = 60 * 1024 * 1024

ROW_SUB = 256
ROWS_PER_ITEM = 1280
FF_TILE = 256
DOWN_TILE = 512
PAGES_PER_STEP = 32

F32 = jnp.float32
BF16 = jnp.bfloat16


def _pick(n, cands):
    for c in cands:
        if n % c == 0:
            return c
    return n


def _cparams(sem, vmem=VMEM_LIMIT):
    return pltpu.CompilerParams(dimension_semantics=sem, vmem_limit_bytes=vmem)


def _mm_kernel(x_ref, w_ref, o_ref):
    o_ref[...] = jnp.dot(x_ref[...], w_ref[...], preferred_element_type=F32).astype(o_ref.dtype)


def _matmul(x, w, tm, tn, out_dtype, name):
    m, k = x.shape
    n = w.shape[1]
    return pl.pallas_call(
        _mm_kernel,
        grid=(m // tm, n // tn),
        in_specs=[pl.BlockSpec((tm, k), lambda i, j: (i, 0)),
                  pl.BlockSpec((k, tn), lambda i, j: (0, j))],
        out_specs=pl.BlockSpec((tm, tn), lambda i, j: (i, j)),
        out_shape=jax.ShapeDtypeStruct((m, n), out_dtype),
        compiler_params=_cparams(("parallel", "arbitrary")),
        name=name,
    )(x, w)


def _mm2_kernel(a_ref, b_ref, wa_ref, wb_ref, o_ref):
    acc = jnp.dot(a_ref[...], wa_ref[...], preferred_element_type=F32)
    acc += jnp.dot(b_ref[...], wb_ref[...], preferred_element_type=F32)
    o_ref[...] = acc


def _matmul2(a, b, wa, wb, tm, tn, name):
    m, ka = a.shape
    kb = b.shape[1]
    n = wa.shape[1]
    return pl.pallas_call(
        _mm2_kernel,
        grid=(m // tm, n // tn),
        in_specs=[pl.BlockSpec((tm, ka), lambda i, j: (i, 0)),
                  pl.BlockSpec((tm, kb), lambda i, j: (i, 0)),
                  pl.BlockSpec((ka, tn), lambda i, j: (0, j)),
                  pl.BlockSpec((kb, tn), lambda i, j: (0, j))],
        out_specs=pl.BlockSpec((tm, tn), lambda i, j: (i, j)),
        out_shape=jax.ShapeDtypeStruct((m, n), F32),
        compiler_params=_cparams(("parallel", "arbitrary")),
        name=name,
    )(a, b, wa, wb)


def _rope128(x, cos2, sin2):
    lane = lax.broadcasted_iota(jnp.int32, x.shape, 1)
    first_half = (lane & (QK_ROPE - 1)) < (QK_ROPE // 2)
    rot = jnp.where(first_half, pltpu.roll(x, LANE - QK_ROPE // 2, 1), pltpu.roll(x, QK_ROPE // 2, 1))
    return x * cos2 + rot * sin2


def _kv_post_kernel(hc_ref, hd_ref, hp_ref, g_ref, cos_ref, sin_ref,
                    ckv_ref, kpe_ref, dk_ref, dv_ref, kall_ref, dkv_ref):
    hc = hc_ref[...]
    ckv = hc * lax.rsqrt(jnp.mean(hc * hc, axis=-1, keepdims=True) + RMS_EPS) * g_ref[...]
    ckv_ref[...] = ckv
    kpe = _rope128(hp_ref[...], cos_ref[...], sin_ref[...])[:, :QK_ROPE]
    kpe_ref[...] = kpe
    kall_ref[:, :KV_LORA] = ckv.astype(BF16)
    kall_ref[:, KV_LORA:] = kpe.astype(BF16)
    hd = hd_ref[...]
    dk_ref[...] = hd[:, :DIFF_V]
    dv_ref[...] = hd[:, DIFF_V:]
    dkv_ref[:, :2 * DIFF_V] = hd.astype(BF16)
    dkv_ref[:, 2 * DIFF_V:] = jnp.ones((hd.shape[0], DKV_WIDTH - 2 * DIFF_V), BF16)


def _kv_post(h2, g_kv, cos2, sin2, q_lora, pe_off, tm):
    t = h2.shape[0]
    row = lambda i: (i, 0)
    return pl.pallas_call(
        _kv_post_kernel,
        grid=(t // tm,),
        in_specs=[pl.BlockSpec((tm, KV_LORA), lambda i: (i, q_lora // KV_LORA)),
                  pl.BlockSpec((tm, 2 * DIFF_V), lambda i: (i, (q_lora + KV_LORA) // (2 * DIFF_V))),
                  pl.BlockSpec((tm, LANE), lambda i: (i, pe_off // LANE)),
                  pl.BlockSpec((1, KV_LORA), lambda i: (0, 0)),
                  pl.BlockSpec((tm, LANE), row),
                  pl.BlockSpec((tm, LANE), row)],
        out_specs=[pl.BlockSpec((tm, KV_LORA), row),
                   pl.BlockSpec((tm, QK_ROPE), row),
                   pl.BlockSpec((tm, DIFF_V), row),
                   pl.BlockSpec((tm, DIFF_V), row),
                   pl.BlockSpec((tm, QK_ALL), row),
                   pl.BlockSpec((tm, DKV_WIDTH), row)],
        out_shape=[jax.ShapeDtypeStruct((t, KV_LORA), F32),
                   jax.ShapeDtypeStruct((t, QK_ROPE), F32),
                   jax.ShapeDtypeStruct((t, DIFF_V), F32),
                   jax.ShapeDtypeStruct((t, DIFF_V), F32),
                   jax.ShapeDtypeStruct((t, QK_ALL), BF16),
                   jax.ShapeDtypeStruct((t, DKV_WIDTH), BF16)],
        compiler_params=_cparams(("parallel",)),
        name="kv_post",
    )(h2, h2, h2, g_kv, cos2, sin2)


def _q_proj_kernel(hq_ref, g_ref, wuq_ref, wk_ref, cos_ref, sin_ref, o_ref, *, n_heads):
    hq = hq_ref[...]
    qn = hq * lax.rsqrt(jnp.mean(hq * hq, axis=-1, keepdims=True) + RMS_EPS) * g_ref[...]
    q = jnp.dot(qn.astype(BF16), wuq_ref[...], preferred_element_type=F32)
    for h in range(n_heads):
        qh = q[:, h * QK_NOPE:(h + 1) * QK_NOPE].astype(BF16)
        o_ref[h, :, :KV_LORA] = jnp.dot(qh, wk_ref[h], preferred_element_type=F32).astype(BF16)
    pe0 = n_heads * QK_NOPE
    cos2, sin2 = cos_ref[...], sin_ref[...]
    for g in range(n_heads // 2):
        r = _rope128(q[:, pe0 + g * LANE: pe0 + (g + 1) * LANE], cos2, sin2).astype(BF16)
        o_ref[2 * g, :, KV_LORA:] = r[:, :QK_ROPE]
        o_ref[2 * g + 1, :, KV_LORA:] = r[:, QK_ROPE:]


def _q_proj(h2, g_q, wuq, wk, cos2, sin2, q_lora, n_heads, tm):
    t = h2.shape[0]
    return pl.pallas_call(
        functools.partial(_q_proj_kernel, n_heads=n_heads),
        grid=(t // tm,),
        in_specs=[pl.BlockSpec((tm, q_lora), lambda i: (i, 0)),
                  pl.BlockSpec((1, q_lora), lambda i: (0, 0)),
                  pl.BlockSpec(wuq.shape, lambda i: (0, 0)),
                  pl.BlockSpec(wk.shape, lambda i: (0, 0, 0)),
                  pl.BlockSpec((tm, LANE), lambda i: (i, 0)),
                  pl.BlockSpec((tm, LANE), lambda i: (i, 0))],
        out_specs=pl.BlockSpec((n_heads, tm, QK_ALL), lambda i: (0, i, 0)),
        out_shape=jax.ShapeDtypeStruct((n_heads, t, QK_ALL), BF16),
        compiler_params=_cparams(("parallel",)),
        name="q_proj",
    )(h2, g_q, wuq, wk, cos2, sin2)


def _sm_step(s, v_bf, m_ref, l_ref, acc_ref):
    m_old = m_ref[...]
    m_new = jnp.maximum(m_old, jnp.max(s, axis=-1, keepdims=True))
    corr = jnp.exp(m_old - m_new)
    p = jnp.exp(s - m_new)
    l_ref[...] = l_ref[...] * corr + jnp.sum(p, axis=-1, keepdims=True)
    acc_ref[...] = acc_ref[...] * corr + jnp.dot(p.astype(BF16), v_bf, preferred_element_type=F32)
    m_ref[...] = m_new


def _sm_reset(m_ref, l_ref, acc_ref):
    m_ref[...] = jnp.full(m_ref.shape, -jnp.inf, F32)
    l_ref[...] = jnp.zeros(l_ref.shape, F32)
    acc_ref[...] = jnp.zeros(acc_ref.shape, F32)


_NT = (((1,), (1,)), ((), ()))


def _mla_prompt_kernel(q_ref, k_ref, wv_ref, o_ref, m_ref, l_ref, acc_ref, *, blk, n_grp):
    i = pl.program_id(1)
    q = q_ref[...].reshape(n_grp * blk, QK_ALL)
    _sm_reset(m_ref, l_ref, acc_ref)

    def logits(j):
        k = k_ref[pl.ds(pl.multiple_of(j * blk, blk), blk), :]
        s = lax.dot_general(q, k, _NT, preferred_element_type=F32) * MLA_SCALE
        return s, k[:, :KV_LORA]

    def body(j, c):
        s, v = logits(j)
        _sm_step(s, v, m_ref, l_ref, acc_ref)
        return c

    lax.fori_loop(0, i, body, 0)
    s, v = logits(i)
    row = lax.broadcasted_iota(jnp.int32, s.shape, 0) & (blk - 1)
    col = lax.broadcasted_iota(jnp.int32, s.shape, 1)
    _sm_step(jnp.where(row >= col, s, -jnp.inf), v, m_ref, l_ref, acc_ref)
    o_lat = (acc_ref[...] / l_ref[...]).astype(BF16)
    for g in range(n_grp):
        o_ref[:, g * V_HEAD:(g + 1) * V_HEAD] = jnp.dot(
            o_lat[g * blk:(g + 1) * blk], wv_ref[g], preferred_element_type=F32).astype(o_ref.dtype)


def _mla_prompt(q_all, kall, wv, batch, seq, blk, n_grp):
    n_heads = q_all.shape[0]
    nq = seq // blk
    rows = n_grp * blk
    return pl.pallas_call(
        functools.partial(_mla_prompt_kernel, blk=blk, n_grp=n_grp),
        grid=(batch, nq, n_heads // n_grp),
        in_specs=[pl.BlockSpec((n_grp, blk, QK_ALL), lambda b, i, h: (h, b * nq + i, 0)),
                  pl.BlockSpec((seq, QK_ALL), lambda b, i, h: (b, 0)),
                  pl.BlockSpec((n_grp, KV_LORA, V_HEAD), lambda b, i, h: (h, 0, 0))],
        out_specs=pl.BlockSpec((blk, n_grp * V_HEAD), lambda b, i, h: (b * nq + i, h)),
        out_shape=jax.ShapeDtypeStruct((batch * seq, n_heads * V_HEAD), BF16),
        scratch_shapes=[pltpu.VMEM((rows, 1), F32), pltpu.VMEM((rows, 1), F32),
                        pltpu.VMEM((rows, KV_LORA), F32)],
        compiler_params=_cparams(("parallel", "parallel", "arbitrary")),
        name="mla_prompt",
    )(q_all, kall, wv)


def _diff_lambda(dl, lam_init):
    s1 = jnp.sum(dl[0:1, :] * dl[1:2, :], axis=-1, keepdims=True)
    s2 = jnp.sum(dl[2:3, :] * dl[3:4, :], axis=-1, keepdims=True)
    return jnp.exp(s1) - jnp.exp(s2) + lam_init


def _diff_merge(o1, o2, lam, g_sub, lam_init):
    d = o1 - lam * o2
    d = d * lax.rsqrt(jnp.mean(d * d, axis=-1, keepdims=True) + SUBLN_EPS) * g_sub
    return d * (1.0 - lam_init)


def _diff_prompt_kernel(q1_ref, q2_ref, kv_ref, bias_ref, far_ref, dl_ref, g_ref, o_ref,
                        m1, a1, m2, a2, *, blk, n_grp, lam_init):
    i = pl.program_id(1)

    def stack(ref):
        x = ref[...].astype(BF16)
        return jnp.concatenate([x[:, g * DIFF_HEAD:(g + 1) * DIFF_HEAD] for g in range(n_grp)], axis=0)

    q1 = stack(q1_ref)
    q2 = stack(q2_ref)
    for m_ref, a_ref in ((m1, a1), (m2, a2)):
        m_ref[...] = jnp.full(m_ref.shape, -jnp.inf, F32)
        a_ref[...] = jnp.zeros(a_ref.shape, F32)

    def step(s, shift, v_aug, m_ref, a_ref):
        m_old = m_ref[...]
        top = jnp.max(s, axis=-1, keepdims=True)
        m_new = jnp.maximum(m_old, top if shift is None else top + shift)
        p = jnp.exp(s - (m_new if shift is None else m_new - shift))
        a_ref[...] = a_ref[...] * jnp.exp(m_old - m_new) + jnp.dot(
            p.astype(BF16), v_aug, preferred_element_type=F32)
        m_ref[...] = m_new

    def logits(j):
        kv = kv_ref[pl.ds(pl.multiple_of(j * blk, blk), blk), :]
        s1 = lax.dot_general(q1, kv[:, :DIFF_HEAD], _NT, preferred_element_type=F32) * DIFF_SCALE
        s2 = lax.dot_general(q2, kv[:, DIFF_HEAD:DIFF_V], _NT, preferred_element_type=F32) * DIFF_SCALE
        return s1, s2, kv[:, DIFF_V:]

    def far_body(j, c):
        s1, s2, v = logits(j)
        far = far_ref[...].reshape(n_grp * blk, LANE)[:, 0:1]
        step(s1, far, v, m1, a1)
        step(s2, far, v, m2, a2)
        return c

    lax.fori_loop(0, i - 1, far_body, 0)

    @pl.when(i >= 1)
    def _():
        s1, s2, v = logits(i - 1)
        b = bias_ref[1].reshape(n_grp * blk, blk)
        step(s1 + b, None, v, m1, a1)
        step(s2 + b, None, v, m2, a2)

    s1, s2, v = logits(i)
    row = lax.broadcasted_iota(jnp.int32, s1.shape, 0) & (blk - 1)
    col = lax.broadcasted_iota(jnp.int32, s1.shape, 1)
    keep = row >= col
    b = bias_ref[0].reshape(n_grp * blk, blk)
    step(jnp.where(keep, s1 + b, -jnp.inf), None, v, m1, a1)
    step(jnp.where(keep, s2 + b, -jnp.inf), None, v, m2, a2)

    def normalised(a_ref):
        a = a_ref[...]
        den = a[:, DIFF_V:]
        return a[:, :DIFF_V] / jnp.concatenate([den] * (DIFF_V // LANE), axis=1)

    lam = _diff_lambda(dl_ref[...], lam_init)
    o = _diff_merge(normalised(a1), normalised(a2), lam, g_ref[...], lam_init).astype(o_ref.dtype)
    for g in range(n_grp):
        o_ref[:, g * DIFF_V:(g + 1) * DIFF_V] = o[g * blk:(g + 1) * blk]


def _diff_prompt(h2, dkv, bias_tiles, far_tiles, dl, g_sub, batch, seq, blk, qd_off, n_dh, n_grp, lam_init):
    nq = seq // blk
    width = n_grp * DIFF_HEAD
    assert qd_off % width == 0 and (n_dh * DIFF_HEAD) % width == 0
    c1 = qd_off // width
    c2 = (qd_off + n_dh * DIFF_HEAD) // width
    rows = n_grp * blk
    return pl.pallas_call(
        functools.partial(_diff_prompt_kernel, blk=blk, n_grp=n_grp, lam_init=lam_init),
        grid=(batch, nq, n_dh // n_grp),
        in_specs=[pl.BlockSpec((blk, width), lambda b, i, h: (b * nq + i, c1 + h)),
                  pl.BlockSpec((blk, width), lambda b, i, h: (b * nq + i, c2 + h)),
                  pl.BlockSpec((seq, DKV_WIDTH), lambda b, i, h: (b, 0)),
                  pl.BlockSpec((2, n_grp, blk, blk), lambda b, i, h: (0, h, 0, 0)),
                  pl.BlockSpec((n_grp, blk, LANE), lambda b, i, h: (h, 0, 0)),
                  pl.BlockSpec((4, DIFF_HEAD), lambda b, i, h: (0, 0)),
                  pl.BlockSpec((1, DIFF_V), lambda b, i, h: (0, 0))],
        out_specs=pl.BlockSpec((blk, n_grp * DIFF_V), lambda b, i, h: (b * nq + i, h)),
        out_shape=jax.ShapeDtypeStruct((batch * seq, n_dh * DIFF_V), BF16),
        scratch_shapes=[pltpu.VMEM((rows, 1), F32), pltpu.VMEM((rows, DIFF_V + LANE), F32),
                        pltpu.VMEM((rows, 1), F32), pltpu.VMEM((rows, DIFF_V + LANE), F32)],
        compiler_params=_cparams(("parallel", "parallel", "arbitrary")),
        name="diff_prompt",
    )(h2, h2, dkv, bias_tiles, far_tiles, dl, g_sub)


def _decode_kernel(pt_ref, qm_ref, qd_ref, ks_ref, dks_ref, bfar_ref, blast_ref, bself_ref,
                   ckv_hbm, kpe_hbm, dk_hbm, dv_hbm,
                   olat_ref, od_ref,
                   ckv_buf, kpe_buf, dk_buf, dv_buf, sems,
                   mm, lm, am, md, ld, ad, *, layer, n_groups, n_batch, gp, page, n_dh):
    b = pl.program_id(0)
    g = pl.program_id(1)
    step = b * n_groups + g
    slot = step % 2

    def copies(bb, gg, sl):
        out = []
        for j in range(gp):
            phys = pt_ref[bb, gg * gp + j]
            out.append(pltpu.make_async_copy(ckv_hbm.at[layer, phys], ckv_buf.at[sl, j], sems.at[sl, 0]))
            out.append(pltpu.make_async_copy(kpe_hbm.at[layer, phys],
                                             kpe_buf.at[sl, :, pl.ds(j * page, page)], sems.at[sl, 1]))
            out.append(pltpu.make_async_copy(dk_hbm.at[layer, phys, :, 0, :],
                                             dk_buf.at[sl, j, :, pl.ds(0, DIFF_HEAD)], sems.at[sl, 2]))
            out.append(pltpu.make_async_copy(dk_hbm.at[layer, phys, :, 1, :],
                                             dk_buf.at[sl, j, :, pl.ds(DIFF_HEAD, DIFF_HEAD)], sems.at[sl, 2]))
            out.append(pltpu.make_async_copy(dv_hbm.at[layer, phys], dv_buf.at[sl, j], sems.at[sl, 3]))
        return out

    @pl.when(step == 0)
    def _():
        for c in copies(0, 0, 0):
            c.start()

    nxt = step + 1

    @pl.when(nxt < n_batch * n_groups)
    def _():
        for c in copies(nxt // n_groups, nxt % n_groups, nxt % 2):
            c.start()

    @pl.when(g == 0)
    def _():
        _sm_reset(mm, lm, am)
        _sm_reset(md, ld, ad)

    for c in copies(b, g, slot):
        c.wait()

    nk = gp * page
    qm = qm_ref[...]
    qd = qd_ref[...]
    kc = ckv_buf[slot].reshape(nk, KV_LORA).astype(BF16)
    kp_t = kpe_buf[slot].astype(BF16)
    s = lax.dot_general(qm[:, :KV_LORA], kc, _NT, preferred_element_type=F32)
    s += jnp.dot(qm[:, KV_LORA:], kp_t, preferred_element_type=F32)
    _sm_step(s * MLA_SCALE, kc, mm, lm, am)

    dk = dk_buf[slot].reshape(nk, DIFF_V).astype(BF16)
    dv = dv_buf[slot].reshape(nk, DIFF_V).astype(BF16)
    s1 = lax.dot_general(qd[:n_dh], dk[:, :DIFF_HEAD], _NT, preferred_element_type=F32)
    s2 = lax.dot_general(qd[n_dh:], dk[:, DIFF_HEAD:], _NT, preferred_element_type=F32)
    bias = jnp.where(g == n_groups - 1, blast_ref[...], bfar_ref[...])
    sd = jnp.concatenate([s1, s2], axis=0) * DIFF_SCALE + bias
    _sm_step(sd, dv, md, ld, ad)

    @pl.when(g == n_groups - 1)
    def _():
        ks = ks_ref[...].astype(F32)
        s_self = jnp.sum(qm.astype(F32) * ks, axis=-1, keepdims=True) * MLA_SCALE
        m_old = mm[...]
        m_new = jnp.maximum(m_old, s_self)
        corr = jnp.exp(m_old - m_new)
        p = jnp.exp(s_self - m_new)
        l_new = lm[...] * corr + p
        olat_ref[...] = (am[...] * corr + p * ks[:, :KV_LORA]) / l_new

        dks = dks_ref[...].astype(F32)
        qdf = qd.astype(F32)
        t1 = jnp.sum(qdf[:n_dh] * dks[:, :DIFF_HEAD], axis=-1, keepdims=True)
        t2 = jnp.sum(qdf[n_dh:] * dks[:, DIFF_HEAD:DIFF_V], axis=-1, keepdims=True)
        sd_self = jnp.concatenate([t1, t2], axis=0) * DIFF_SCALE + bself_ref[:, 0:1]
        m_old = md[...]
        m_new = jnp.maximum(m_old, sd_self)
        corr = jnp.exp(m_old - m_new)
        p = jnp.exp(sd_self - m_new)
        l_new = ld[...] * corr + p
        od_ref[...] = (ad[...] * corr + p * dks[:, DIFF_V:]) / l_new


def _decode(page_table, qm, qd, kall_s, dkv_s, bfar, blast, bself,
            cache_ckv, cache_kpe, cache_dk, cache_dv, layer, gp):
    n_batch, n_heads, _ = qm.shape
    n_maps = qd.shape[1]
    n_pages = page_table.shape[1]
    page = cache_ckv.shape[2]
    n_groups = n_pages // gp
    nk = gp * page
    per_b = lambda b, g, pt: (b, 0, 0)
    const2 = lambda b, g, pt: (0, 0)
    any_spec = pl.BlockSpec(memory_space=pl.ANY)
    grid_spec = pltpu.PrefetchScalarGridSpec(
        num_scalar_prefetch=1,
        grid=(n_batch, n_groups),
        in_specs=[pl.BlockSpec((None, n_heads, QK_ALL), per_b),
                  pl.BlockSpec((None, n_maps, DIFF_HEAD), per_b),
                  pl.BlockSpec((None, 1, QK_ALL), per_b),
                  pl.BlockSpec((None, 1, 2 * DIFF_V), per_b),
                  pl.BlockSpec((n_maps, nk), const2),
                  pl.BlockSpec((n_maps, nk), const2),
                  pl.BlockSpec((n_maps, LANE), const2),
                  any_spec, any_spec, any_spec, any_spec],
        out_specs=[pl.BlockSpec((None, n_heads, KV_LORA), per_b),
                   pl.BlockSpec((None, n_maps, DIFF_V), per_b)],
        scratch_shapes=[pltpu.VMEM((2, gp, page, KV_LORA), F32),
                        pltpu.VMEM((2, QK_ROPE, gp * page), F32),
                        pltpu.VMEM((2, gp, page, DIFF_V), F32),
                        pltpu.VMEM((2, gp, page, DIFF_V), F32),
                        pltpu.SemaphoreType.DMA((2, 4)),
                        pltpu.VMEM((n_heads, 1), F32), pltpu.VMEM((n_heads, 1), F32),
                        pltpu.VMEM((n_heads, KV_LORA), F32),
                        pltpu.VMEM((n_maps, 1), F32), pltpu.VMEM((n_maps, 1), F32),
                        pltpu.VMEM((n_maps, DIFF_V), F32)])
    return pl.pallas_call(
        functools.partial(_decode_kernel, layer=layer, n_groups=n_groups, n_batch=n_batch, gp=gp,
                          page=page, n_dh=n_maps // 2),
        grid_spec=grid_spec,
        out_shape=[jax.ShapeDtypeStruct((n_batch, n_heads, KV_LORA), F32),
                   jax.ShapeDtypeStruct((n_batch, n_maps, DIFF_V), F32)],
        compiler_params=_cparams(("arbitrary", "arbitrary")),
        name="decode_attn",
    )(page_table, qm, qd, kall_s, dkv_s, bfar, blast, bself, cache_ckv, cache_kpe, cache_dk, cache_dv)


def _decode_merge_kernel(olat_ref, wv_ref, od_ref, dl_ref, g_ref, omla_ref, odiff_ref, *, n_dh, lam_init):
    h = pl.program_id(0)
    omla_ref[...] = jnp.dot(olat_ref[...].astype(BF16), wv_ref[...],
                            preferred_element_type=F32).astype(omla_ref.dtype)

    @pl.when(h < n_dh)
    def _():
        lam = _diff_lambda(dl_ref[...], lam_init)
        odiff_ref[...] = _diff_merge(od_ref[0], od_ref[1], lam, g_ref[...], lam_init).astype(odiff_ref.dtype)


def _decode_merge(olat_h, wv, od_h, dl, g_sub, lam_init):
    n_heads, n_batch, _ = olat_h.shape
    n_dh = od_h.shape[1]
    return pl.pallas_call(
        functools.partial(_decode_merge_kernel, n_dh=n_dh, lam_init=lam_init),
        grid=(n_heads,),
        in_specs=[pl.BlockSpec((None, n_batch, KV_LORA), lambda h: (h, 0, 0)),
                  pl.BlockSpec((None, KV_LORA, V_HEAD), lambda h: (h, 0, 0)),
                  pl.BlockSpec((2, None, n_batch, DIFF_V), lambda h: (0, jnp.minimum(h, n_dh - 1), 0, 0)),
                  pl.BlockSpec((4, DIFF_HEAD), lambda h: (0, 0)),
                  pl.BlockSpec((1, DIFF_V), lambda h: (0, 0))],
        out_specs=[pl.BlockSpec((n_batch, V_HEAD), lambda h: (0, h)),
                   pl.BlockSpec((n_batch, DIFF_V), lambda h: (0, jnp.minimum(h, n_dh - 1)))],
        out_shape=[jax.ShapeDtypeStruct((n_batch, n_heads * V_HEAD), BF16),
                   jax.ShapeDtypeStruct((n_batch, n_dh * DIFF_V), BF16)],
        compiler_params=_cparams(("arbitrary",)),
        name="decode_merge",
    )(olat_h, wv, od_h, dl, g_sub)


def _layernorm(z, g, b):
    mu = jnp.mean(z, axis=-1, keepdims=True)
    zc = z - mu
    var = jnp.mean(zc * zc, axis=-1, keepdims=True)
    return zc * lax.rsqrt(var + LN_EPS) * g + b


def _ln_router_kernel(z_ref, x_ref, g_ref, b_ref, wr_ref, br_ref, h_ref, hpk_ref, ids_ref, gates_ref,
                      *, alpha, n_experts):
    hmid = _layernorm(alpha * x_ref[...] + z_ref[...], g_ref[...], b_ref[...])
    h_ref[...] = hmid
    hbf = hmid.astype(BF16)
    bits = lax.bitcast_convert_type(hbf.astype(F32), jnp.uint32)
    half = bits.shape[1] // 2
    hpk_ref[...] = (bits[:, :half] >> 16) | (bits[:, half:] & jnp.uint32(0xFFFF0000))
    logits = jnp.dot(hbf, wr_ref[...], preferred_element_type=F32) + br_ref[...]
    lane = lax.broadcasted_iota(jnp.int32, logits.shape, 1)
    out_lane = lax.broadcasted_iota(jnp.int32, ids_ref.shape, 1)
    ids = jnp.zeros(ids_ref.shape, jnp.int32)
    vals = jnp.zeros(gates_ref.shape, F32)
    top = None
    denom = None
    for k in range(TOP_K):
        v = jnp.max(logits, axis=-1, keepdims=True)
        idx = jnp.min(jnp.where(logits == v, lane, n_experts), axis=-1, keepdims=True)
        logits = jnp.where(lane == idx, -jnp.inf, logits)
        if k == 0:
            top = v
        e = jnp.exp(v - top)
        denom = e if k == 0 else denom + e
        ids = jnp.where(out_lane == k, idx, ids)
        vals = jnp.where(out_lane == k, e, vals)
    ids_ref[...] = ids
    gates_ref[...] = vals / denom


def _ln_router(z, x, g, b, wr, br, alpha, tm):
    t, d = z.shape
    n_experts = wr.shape[1]
    row = lambda i: (i, 0)
    c = lambda i: (0, 0)
    return pl.pallas_call(
        functools.partial(_ln_router_kernel, alpha=alpha, n_experts=n_experts),
        grid=(t // tm,),
        in_specs=[pl.BlockSpec((tm, d), row), pl.BlockSpec((tm, d), row),
                  pl.BlockSpec((1, d), c), pl.BlockSpec((1, d), c),
                  pl.BlockSpec((d, n_experts), c), pl.BlockSpec((1, n_experts), c)],
        out_specs=[pl.BlockSpec((tm, d), row), pl.BlockSpec((tm, d // 2), row),
                   pl.BlockSpec((tm, LANE), row), pl.BlockSpec((tm, LANE), row)],
        out_shape=[jax.ShapeDtypeStruct((t, d), F32), jax.ShapeDtypeStruct((t, d // 2), jnp.uint32),
                   jax.ShapeDtypeStruct((t, LANE), jnp.int32), jax.ShapeDtypeStruct((t, LANE), F32)],
        compiler_params=_cparams(("parallel",)),
        name="ln_router",
    )(z, x, g, b, wr, br)


def _moe_kernel(ex_ref, r0_ref, ns_ref, tok_ref, used_ref,
                hpk_hbm, perm_ref, wgu_ref, bgu_ref, wd_ref, bd_ref, ys_hbm,
                xraw, act, wbf, ybuf, xsem, ysem, *, n_ff, n_down, n_items, chunk):
    w = pl.program_id(0)
    s = pl.program_id(1)
    nsub = ns_ref[w]
    row0 = r0_ref[w]
    half = xraw.shape[1]
    nxt = jnp.minimum(w + 1, n_items - 1)
    chunks_per_sub = ROW_SUB // chunk

    def issue_chunk(base, k):
        for u in range(chunk):
            r = k * chunk + u
            pltpu.make_async_copy(hpk_hbm.at[pl.ds(tok_ref[base + r], 1)], xraw.at[pl.ds(r, 1)],
                                  xsem.at[0]).start()

    def issue_chunks(base, lo, hi):
        def go(k, c):
            issue_chunk(base, k)
            return c
        lax.fori_loop(lo, hi, go, 0)

    def wait_chunks(n):
        def go(k, c):
            pltpu.make_async_copy(hpk_hbm.at[pl.ds(0, chunk)], xraw.at[pl.ds(0, chunk)], xsem.at[0]).wait()
            return c
        lax.fori_loop(0, n, go, 0)

    @pl.when(s == 0)
    def _():
        @pl.when(w == 0)
        def _():
            issue_chunks(row0, 0, nsub * chunks_per_sub)
            wait_chunks(nsub * chunks_per_sub)

        @pl.when(w > 0)
        def _():
            wait_chunks(jnp.maximum(ns_ref[jnp.maximum(w - 1, 0)], nsub) * chunks_per_sub)

    @pl.when(s < n_ff)
    def _():
        wbf[...] = wgu_ref[...].astype(BF16)
        bias = bgu_ref[...]
        perm = perm_ref[...]
        lane = lax.broadcasted_iota(jnp.int32, (ROW_SUB, LANE), 1)
        even = (lane & 1) == 0

        def sub(r, c):
            rows = pl.ds(pl.multiple_of(r * ROW_SUB, ROW_SUB), ROW_SUB)
            xr = xraw[rows, :]
            x_lo = lax.bitcast_convert_type(xr << 16, F32).astype(BF16)
            x_hi = lax.bitcast_convert_type(xr & jnp.uint32(0xFFFF0000), F32).astype(BF16)
            hgu = jnp.dot(x_lo, wbf[pl.ds(0, half), :], preferred_element_type=F32)
            hgu = hgu + jnp.dot(x_hi, wbf[pl.ds(half, half), :], preferred_element_type=F32) + bias
            for q in range(FF_TILE // LANE):
                a = hgu[:, (2 * q) * LANE:(2 * q + 1) * LANE]
                bq = hgu[:, (2 * q + 1) * LANE:(2 * q + 2) * LANE]
                gate = jnp.where(even, a, pltpu.roll(bq, 1, 1))
                up = jnp.where(even, pltpu.roll(a, LANE - 1, 1), bq)
                gate = jnp.minimum(gate, SWIGLU_LIMIT)
                up = jnp.clip(up, -SWIGLU_LIMIT, SWIGLU_LIMIT)
                mixed = ((up + 1.0) * gate * jax.nn.sigmoid(SWIGLU_ALPHA * gate)).astype(BF16)
                act[s, rows, q * LANE:(q + 1) * LANE] = jnp.dot(
                    mixed, perm, preferred_element_type=F32).astype(BF16)
            return c
        lax.fori_loop(0, nsub, sub, 0)

    def y_copy(first_row, n, sl):
        dst = ys_hbm.at[pl.ds(pl.multiple_of(first_row, ROW_SUB), ROW_SUB),
                        pl.ds(pl.multiple_of(n * DOWN_TILE, DOWN_TILE), DOWN_TILE)]
        return pltpu.make_async_copy(ybuf.at[sl], dst, ysem.at[sl])

    @pl.when(s >= n_ff)
    def _():
        n = s - n_ff
        wd_bf = wbf.at[pl.ds(0, n_ff * FF_TILE)]
        wd_bf[...] = wd_ref[...].astype(BF16)
        bias = bd_ref[...]
        nxt_row0 = r0_ref[nxt]

        def sub(r, c):
            rows = pl.ds(pl.multiple_of(r * ROW_SUB, ROW_SUB), ROW_SUB)
            sl = r % 2
            issue_chunk(nxt_row0, n * nsub + r)

            @pl.when(r >= 2)
            def _():
                y_copy(row0, n, sl).wait()
            y = bias
            for j in range(n_ff):
                y = y + jnp.dot(act[j, rows, :], wd_bf[pl.ds(j * FF_TILE, FF_TILE), :],
                                preferred_element_type=F32)
            ybuf[sl] = y
            y_copy(row0 + r * ROW_SUB, n, sl).start()
            return c
        lax.fori_loop(0, nsub, sub, 0)

        @pl.when(nsub >= 2)
        def _():
            y_copy(row0, n, nsub % 2).wait()

        @pl.when(nsub >= 1)
        def _():
            y_copy(row0, n, (nsub - 1) % 2).wait()

        @pl.when(s == n_ff + n_down - 1)
        def _():
            issue_chunks(nxt_row0, nsub * n_down, ns_ref[nxt] * n_down)

    @pl.when((w == n_items - 1) & (s == n_ff + n_down - 1))
    def _():
        ybuf[0] = jnp.zeros((ROW_SUB, DOWN_TILE), F32)
        first = used_ref[0]
        total = ys_hbm.shape[0] // ROW_SUB

        def fill(tile, c):
            for n in range(n_down):
                y_copy(tile * ROW_SUB, n, 0).start()
            return c
        lax.fori_loop(first, total, fill, 0)

        def drain(tile, c):
            for n in range(n_down):
                y_copy(tile * ROW_SUB, n, 0).wait()
            return c
        lax.fori_loop(first, total, drain, 0)


def _moe_ffn(item_expert, item_row0, item_nsub, tok_of_row, n_used, hpk, perm,
             w_gate_up, b_gate_up, w_down, b_down, layer, n_rows):
    d_model = 2 * hpk.shape[1]
    n_experts, _, two_de = w_gate_up.shape[1:]
    d_expert = two_de // 2
    n_ff = d_expert // FF_TILE
    n_down = d_model // DOWN_TILE
    n_items = item_expert.shape[0]
    chunk = ROW_SUB // n_down
    bgu = b_gate_up.reshape(b_gate_up.shape[0], n_experts, 1, two_de)
    bd = b_down.reshape(b_down.shape[0], n_experts, 1, d_model)

    def wgu_map(w, s, ex, r0, ns, tok, used):
        return (layer, ex[w], 0, jnp.where(ns[w] > 0, jnp.minimum(s, n_ff - 1), n_ff - 1))

    def wd_map(w, s, ex, r0, ns, tok, used):
        return (layer, ex[w], 0, jnp.where(ns[w] > 0, jnp.maximum(s - n_ff, 0), n_down - 1))

    grid_spec = pltpu.PrefetchScalarGridSpec(
        num_scalar_prefetch=5,
        grid=(n_items, n_ff + n_down),
        in_specs=[pl.BlockSpec(memory_space=pl.ANY),
                  pl.BlockSpec((LANE, LANE), lambda w, s, *_: (0, 0)),
                  pl.BlockSpec((None, None, d_model, 2 * FF_TILE), wgu_map),
                  pl.BlockSpec((None, None, 1, 2 * FF_TILE), wgu_map),
                  pl.BlockSpec((None, None, d_expert, DOWN_TILE), wd_map),
                  pl.BlockSpec((None, None, 1, DOWN_TILE), wd_map)],
        out_specs=pl.BlockSpec(memory_space=pl.ANY),
        scratch_shapes=[pltpu.VMEM((ROWS_PER_ITEM, d_model // 2), jnp.uint32),
                        pltpu.VMEM((n_ff, ROWS_PER_ITEM, FF_TILE), BF16),
                        pltpu.VMEM((max(d_model, d_expert), 2 * FF_TILE), BF16),
                        pltpu.VMEM((2, ROW_SUB, DOWN_TILE), F32),
                        pltpu.SemaphoreType.DMA((1,)),
                        pltpu.SemaphoreType.DMA((2,))])
    return pl.pallas_call(
        functools.partial(_moe_kernel, n_ff=n_ff, n_down=n_down, n_items=n_items, chunk=chunk),
        grid_spec=grid_spec,
        out_shape=jax.ShapeDtypeStruct((n_rows, d_model), F32),
        compiler_params=_cparams(("arbitrary", "arbitrary")),
        name="moe_ffn",
    )(item_expert, item_row0, item_nsub, tok_of_row, n_used, hpk, perm, w_gate_up, bgu, w_down, bd)


def _combine_ln_kernel(y_ref, gates_ref, h_ref, g_ref, b_ref, op_ref, os_ref, *, alpha, n_prompt_tiles):
    gates = gates_ref[...]
    ffn = gates[:, 0:1] * y_ref[0]
    for k in range(1, TOP_K):
        ffn = ffn + gates[:, k:k + 1] * y_ref[k]
    out = _layernorm(alpha * h_ref[...] + ffn, g_ref[...], b_ref[...])
    i = pl.program_id(0)

    @pl.when(i < n_prompt_tiles)
    def _():
        op_ref[...] = out

    @pl.when(i >= n_prompt_tiles)
    def _():
        os_ref[...] = out


def _combine_ln(yk, gates, hmid, g, b, alpha, tm, tp):
    t, d = hmid.shape
    assert tp % tm == 0 and (t - tp) % tm == 0
    npt = tp // tm
    row = lambda i: (i, 0)
    c = lambda i: (0, 0)
    return pl.pallas_call(
        functools.partial(_combine_ln_kernel, alpha=alpha, n_prompt_tiles=npt),
        grid=(t // tm,),
        in_specs=[pl.BlockSpec((TOP_K, tm, d), lambda i: (0, i, 0)),
                  pl.BlockSpec((tm, LANE), row), pl.BlockSpec((tm, d), row),
                  pl.BlockSpec((1, d), c), pl.BlockSpec((1, d), c)],
        out_specs=[pl.BlockSpec((tm, d), lambda i: (jnp.minimum(i, npt - 1), 0)),
                   pl.BlockSpec((tm, d), lambda i: (jnp.maximum(i - npt, 0), 0))],
        out_shape=[jax.ShapeDtypeStruct((tp, d), F32), jax.ShapeDtypeStruct((t - tp, d), F32)],
        compiler_params=_cparams(("arbitrary",)),
        name="combine_ln",
    )(yk, gates, hmid, g, b)


def _rope_tables(pos):
    inv = ROPE_THETA ** (-jnp.arange(0, QK_ROPE, 2, dtype=F32) / QK_ROPE)
    ang = pos.astype(F32)[:, None] * inv[None, :]
    cos, sin = jnp.cos(ang), jnp.sin(ang)
    return jnp.tile(cos, (1, 4)), jnp.tile(jnp.concatenate([-sin, sin], axis=-1), (1, 2))


def _t5_bucket(dist):
    n = jnp.maximum(dist, 0)
    max_exact = N_BUCKETS // 2
    large = max_exact + (jnp.log(jnp.maximum(n, max_exact).astype(F32) / max_exact)
                         / math.log(MAX_DISTANCE / max_exact) * (N_BUCKETS - max_exact)).astype(jnp.int32)
    return jnp.where(n < max_exact, n, jnp.minimum(large, N_BUCKETS - 1))


def _bias_tiles(rb, blk):
    n_dh = rb.shape[1]
    period = 2 * blk
    tiles = []
    for tt in range(2):
        e = jnp.arange(period, dtype=jnp.int32)
        e = jnp.where(e < blk, e, e - period)
        v = rb[_t5_bucket(tt * blk - e)].T
        x = jnp.tile(v, (1, blk))[:, :blk * (period - 1)].reshape(n_dh, blk, period - 1)
        tiles.append(x[:, :, :blk])
    return jnp.stack(tiles)


def _route_layout(ids, n_experts, n_items, n_rows):
    t = ids.shape[0]
    flat = ids.reshape(-1)
    onehot = (flat[:, None] == jnp.arange(n_experts, dtype=jnp.int32)[None, :]).astype(jnp.int32)
    csum = jnp.cumsum(onehot, axis=0)
    counts = csum[-1]
    rank = jnp.take_along_axis(csum, flat[:, None], axis=1)[:, 0] - 1
    nsub_e = (counts + ROW_SUB - 1) // ROW_SUB
    seg_start = (jnp.cumsum(nsub_e) - nsub_e) * ROW_SUB
    slot = seg_start[flat] + rank
    tok_of_row = jnp.zeros((n_rows + ROWS_PER_ITEM,), jnp.int32).at[slot].set(
        jnp.arange(t * TOP_K, dtype=jnp.int32) // TOP_K, unique_indices=True)
    n_used = jnp.sum(nsub_e).astype(jnp.int32).reshape(1)
    sub_per_item = ROWS_PER_ITEM // ROW_SUB
    assert n_items * sub_per_item > n_rows // ROW_SUB + (sub_per_item - 1) * n_experts, "last item must be empty"
    items_e = (nsub_e + sub_per_item - 1) // sub_per_item
    item_end = jnp.cumsum(items_e)
    item_start = item_end - items_e
    w = jnp.arange(n_items, dtype=jnp.int32)
    n_valid = item_end[-1]
    e_of_w = jnp.minimum(jnp.searchsorted(item_end, w, side="right"), n_experts - 1).astype(jnp.int32)
    chunk = w - item_start[e_of_w]
    nsub_w = jnp.clip(nsub_e[e_of_w] - chunk * sub_per_item, 0, sub_per_item)
    valid = w < n_valid
    last_e = e_of_w[jnp.maximum(n_valid - 1, 0)]
    item_expert = jnp.where(valid, e_of_w, last_e).astype(jnp.int32)
    item_nsub = jnp.where(valid, nsub_w, 0).astype(jnp.int32)
    item_row0 = jnp.where(valid, seg_start[e_of_w] + chunk * ROWS_PER_ITEM, 0).astype(jnp.int32)
    return slot.reshape(t, TOP_K), tok_of_row, n_used, item_expert, item_row0, item_nsub


def _act_perm():
    n = np.arange(LANE)
    src = np.where(n < LANE // 2, 2 * n, 2 * (n - LANE // 2) + 1)
    p = np.zeros((LANE, LANE), np.float32)
    p[src, n] = 1.0
    return jnp.asarray(p, BF16)


def kernel(x_prompt, x_sample, cache_ckv, cache_kpe, cache_dk, cache_dv, page_table, rel_bias,
           w_in, g_q, w_uq, g_kv, w_ukv, diff_lambda, g_sub, w_out, ln1_g, ln1_b,
           w_router, b_router, w_gate_up, b_gate_up, w_down, b_down, ln2_g, ln2_b):
    bp, sp, d = x_prompt.shape
    bs, ss, _ = x_sample.shape
    assert ss == 1, "decode path handles one new token per sequence"
    depth = w_in.shape[0]
    n_heads = (d // 2) // V_HEAD
    n_dh = (d // 2) // DIFF_V
    q_lora = w_uq.shape[1]
    n_experts = w_router.shape[2]
    page = cache_ckv.shape[2]
    n_pages = page_table.shape[1]
    past_len = n_pages * page
    tp, ts = bp * sp, bs * ss
    t = tp + ts
    dn_alpha = (2 * depth) ** 0.25

    c_ckv = q_lora
    c_dkv = c_ckv + KV_LORA
    c_qd = c_dkv + 2 * DIFF_V
    c_pe = c_qd + 2 * n_dh * DIFF_HEAD
    n_in = c_pe + LANE
    assert c_ckv % KV_LORA == 0 and c_dkv % (2 * DIFF_V) == 0
    assert (t * TOP_K) % ROW_SUB == 0
    i1 = q_lora
    i2 = i1 + KV_LORA + QK_ROPE
    i3 = i2 + 2 * n_dh * DIFF_HEAD
    i4 = i3 + 2 * DIFF_HEAD

    tm_tok = _pick(t, (640, 320, 256, 128, 64, 32, 16, 8))
    tm_big = _pick(t, (1664, 640, 320, 256, 128, 64, 32, 16, 8))
    blk = _pick(sp, (256, 128))
    gp = _pick(n_pages, (PAGES_PER_STEP, 4, 2, 1))
    assert sp % blk == 0 and blk >= MAX_DISTANCE and page >= MAX_DISTANCE

    pos_p = jnp.arange(sp, dtype=jnp.int32)
    pos_all = jnp.concatenate([jnp.tile(pos_p, bp), jnp.full((ts,), past_len, jnp.int32)])
    cos2, sin2 = _rope_tables(pos_all)

    far_idx = _t5_bucket(jnp.asarray(2 * blk, jnp.int32))
    offs = jnp.arange(page, dtype=jnp.int32)
    last_idx = _t5_bucket(past_len - ((n_pages - 1) * page + offs))
    far_dec_idx = _t5_bucket(jnp.asarray(past_len - ((n_pages - 1) * page - 1), jnp.int32))

    x_all = jnp.concatenate([x_prompt.reshape(tp, d), x_sample.reshape(ts, d)], axis=0)
    perm = _act_perm()
    kpe_t = jnp.swapaxes(cache_kpe, 2, 3)
    new_p, new_s = [], []
    for l in range(depth):
        lam_init = 0.8 - 0.6 * math.exp(-0.3 * l)
        wi = w_in[l]
        w_in2 = jnp.concatenate(
            [wi[:, :i1], wi[:, i1:i1 + KV_LORA], wi[:, i3:i4], wi[:, i4:], wi[:, i2:i3],
             wi[:, i1 + KV_LORA:i2], jnp.zeros((d, LANE - QK_ROPE), wi.dtype)], axis=1).astype(BF16)
        wuq = w_uq[l].reshape(q_lora, n_heads, QK_NOPE + QK_ROPE)
        wuq2 = jnp.concatenate([wuq[:, :, :QK_NOPE].reshape(q_lora, -1),
                                wuq[:, :, QK_NOPE:].reshape(q_lora, -1)], axis=1).astype(BF16)
        wk = jnp.transpose(w_ukv[l][:, :, :QK_NOPE], (1, 2, 0)).astype(BF16)
        wv = jnp.transpose(w_ukv[l][:, :, QK_NOPE:], (1, 0, 2)).astype(BF16)
        wo = w_out[l].astype(BF16)
        half = n_heads * V_HEAD

        x_bf = x_all.astype(BF16)
        h2 = _matmul(x_bf, w_in2, tm_big, _pick(n_in, (384, 256, 128)), F32, "in_proj")
        ckv, kpe, dk, dv, kall, dkv = _kv_post(h2, g_kv[l][None, :], cos2, sin2, q_lora, c_pe, tm_tok)
        q_all = _q_proj(h2, g_q[l][None, :], wuq2, wk, cos2, sin2, q_lora, n_heads,
                        _pick(t, (320, 256, 128, 64, 32, 16, 8)))

        rb = rel_bias.astype(F32)
        bias_tiles = _bias_tiles(rb, blk)
        far_tiles = jnp.broadcast_to(rb[far_idx][:, None, None], (n_dh, blk, LANE))
        dl = diff_lambda[l].astype(F32)
        gs = g_sub[l][None, :].astype(F32)
        mix_mla_p = _mla_prompt(q_all, kall, wv, bp, sp, blk, _pick(n_heads, (4, 2, 1)))
        mix_diff_p = _diff_prompt(h2, dkv, bias_tiles, far_tiles, dl, gs, bp, sp, blk, c_qd, n_dh,
                                  _pick(n_dh, (4, 2, 1)), lam_init)

        qm_s = jnp.transpose(q_all[:, tp:, :], (1, 0, 2))
        qd_s = h2[tp:, c_qd:c_qd + 2 * n_dh * DIFF_HEAD].astype(BF16).reshape(ts, 2 * n_dh, DIFF_HEAD)
        nk = gp * page
        bfar = jnp.broadcast_to(jnp.tile(rb[far_dec_idx], 2)[:, None], (2 * n_dh, nk))
        blast = jnp.concatenate([bfar[:, :nk - page], jnp.tile(rb[last_idx].T, (2, 1))], axis=1)
        bself = jnp.broadcast_to(jnp.tile(rb[0], 2)[:, None], (2 * n_dh, LANE))
        olat, od = _decode(page_table, qm_s, qd_s, kall[tp:, None, :], dkv[tp:, None, :],
                           bfar, blast, bself, cache_ckv, kpe_t, cache_dk, cache_dv, l, gp)
        od_h = jnp.transpose(od.reshape(ts, 2, n_dh, DIFF_V), (1, 2, 0, 3))
        mix_mla_s, mix_diff_s = _decode_merge(jnp.transpose(olat, (1, 0, 2)), wv, od_h, dl, gs, lam_init)

        mix_a = jnp.concatenate([mix_mla_p, mix_mla_s], axis=0)
        mix_b = jnp.concatenate([mix_diff_p, mix_diff_s], axis=0)
        z = _matmul2(mix_a, mix_b, wo[:half], wo[half:], tm_big, _pick(d, (512, 256, 128)), "out_proj")
        hmid, hpk, ids128, gates128 = _ln_router(
            z, x_all, ln1_g[l][None, :], ln1_b[l][None, :], w_router[l].astype(BF16),
            b_router[l][None, :].astype(F32), dn_alpha, _pick(t, (320, 256, 128, 64, 32, 16, 8)))

        n_rows = t * TOP_K + n_experts * ROW_SUB
        n_items = n_experts + (n_rows + ROWS_PER_ITEM - 1) // ROWS_PER_ITEM
        slot, tok_of_row, n_used, item_expert, item_row0, item_nsub = _route_layout(
            ids128[:, :TOP_K], n_experts, n_items, n_rows)
        ys = _moe_ffn(item_expert, item_row0, item_nsub, tok_of_row, n_used, hpk, perm,
                      w_gate_up, b_gate_up, w_down, b_down, l, n_rows)
        yk = ys.at[slot.T].get(mode="promise_in_bounds")
        out_p, out_s = _combine_ln(yk, gates128, hmid, ln2_g[l][None, :], ln2_b[l][None, :], dn_alpha,
                                   _pick(math.gcd(tp, ts), (128, 64, 32, 16, 8)), tp)
        if l + 1 < depth:
            x_all = jnp.concatenate([out_p, out_s], axis=0)
        new_p.append((ckv[:tp].reshape(bp, sp, KV_LORA), kpe[:tp].reshape(bp, sp, QK_ROPE),
                      dk[:tp].reshape(bp, sp, 2, DIFF_HEAD), dv[:tp].reshape(bp, sp, DIFF_V)))
        new_s.append((ckv[tp:].reshape(bs, ss, KV_LORA), kpe[tp:].reshape(bs, ss, QK_ROPE),
                      dk[tp:].reshape(bs, ss, 2, DIFF_HEAD), dv[tp:].reshape(bs, ss, DIFF_V)))

    stack = lambda rows, i: jnp.stack([r_[i] for r_ in rows])
    return (out_p.reshape(bp, sp, d), out_s.reshape(bs, ss, d),
            stack(new_p, 0), stack(new_p, 1), stack(new_p, 2), stack(new_p, 3),
            stack(new_s, 0), stack(new_s, 1), stack(new_s, 2), stack(new_s, 3))
```

```python
import functools
import math

import jax
import jax.numpy as jnp
import numpy as np
from jax import lax
from jax.experimental import pallas as pl
from jax.experimental.pallas import tpu as pltpu

V_HEAD = 128
KV_LORA = 512
QK_NOPE = 128
QK_ROPE = 64
QK_ALL = KV_LORA + QK_ROPE
ROPE_THETA = 10000.0
MLA_SCALE = (QK_NOPE + QK_ROPE) ** -0.5
DIFF_HEAD = 128
DIFF_V = 2 * DIFF_HEAD
DIFF_SCALE = DIFF_HEAD ** -0.5
DKV_WIDTH = 2 * DIFF_V + 128
SUBLN_EPS = 1e-5
N_BUCKETS = 32
MAX_DISTANCE = 128
TOP_K = 4
SWIGLU_LIMIT = 7.0
SWIGLU_ALPHA = 1.702
LN_EPS = 1e-5
RMS_EPS = 1e-6

LANE = 128
VMEM_LIMIT = 60 * 1024 * 1024

ROW_SUB = 256
ROWS_PER_ITEM = 1280
FF_TILE = 256
DOWN_TILE = 512
PAGES_PER_STEP = 32

F32 = jnp.float32
BF16 = jnp.bfloat16


def _pick(n, cands):
    for c in cands:
        if n % c == 0:
            return c
    return n


def _cparams(sem, vmem=VMEM_LIMIT):
    return pltpu.CompilerParams(dimension_semantics=sem, vmem_limit_bytes=vmem)


def _mm_kernel(x_ref, w_ref, o_ref):
    o_ref[...] = jnp.dot(x_ref[...], w_ref[...], preferred_element_type=F32).astype(o_ref.dtype)


def _matmul(x, w, tm, tn, out_dtype, name):
    m, k = x.shape
    n = w.shape[1]
    return pl.pallas_call(
        _mm_kernel,
        grid=(m // tm, n // tn),
        in_specs=[pl.BlockSpec((tm, k), lambda i, j: (i, 0)),
                  pl.BlockSpec((k, tn), lambda i, j: (0, j))],
        out_specs=pl.BlockSpec((tm, tn), lambda i, j: (i, j)),
        out_shape=jax.ShapeDtypeStruct((m, n), out_dtype),
        compiler_params=_cparams(("parallel", "arbitrary")),
        name=name,
    )(x, w)


def _mm2_kernel(a_ref, b_ref, wa_ref, wb_ref, o_ref):
    acc = jnp.dot(a_ref[...], wa_ref[...], preferred_element_type=F32)
    acc += jnp.dot(b_ref[...], wb_ref[...], preferred_element_type=F32)
    o_ref[...] = acc


def _matmul2(a, b, wa, wb, tm, tn, name):
    m, ka = a.shape
    kb = b.shape[1]
    n = wa.shape[1]
    return pl.pallas_call(
        _mm2_kernel,
        grid=(m // tm, n // tn),
        in_specs=[pl.BlockSpec((tm, ka), lambda i, j: (i, 0)),
                  pl.BlockSpec((tm, kb), lambda i, j: (i, 0)),
                  pl.BlockSpec((ka, tn), lambda i, j: (0, j)),
                  pl.BlockSpec((kb, tn), lambda i, j: (0, j))],
        out_specs=pl.BlockSpec((tm, tn), lambda i, j: (i, j)),
        out_shape=jax.ShapeDtypeStruct((m, n), F32),
        compiler_params=_cparams(("parallel", "arbitrary")),
        name=name,
    )(a, b, wa, wb)


def _rope128(x, cos2, sin2):
    lane = lax.broadcasted_iota(jnp.int32, x.shape, 1)
    first_half = (lane & (QK_ROPE - 1)) < (QK_ROPE // 2)
    rot = jnp.where(first_half, pltpu.roll(x, LANE - QK_ROPE // 2, 1), pltpu.roll(x, QK_ROPE // 2, 1))
    return x * cos2 + rot * sin2


def _kv_post_kernel(hc_ref, hd_ref, hp_ref, g_ref, cos_ref, sin_ref,
                    ckv_ref, kpe_ref, dk_ref, dv_ref, kall_ref, dkv_ref):
    hc = hc_ref[...]
    ckv = hc * lax.rsqrt(jnp.mean(hc * hc, axis=-1, keepdims=True) + RMS_EPS) * g_ref[...]
    ckv_ref[...] = ckv
    kpe = _rope128(hp_ref[...], cos_ref[...], sin_ref[...])[:, :QK_ROPE]
    kpe_ref[...] = kpe
    kall_ref[:, :KV_LORA] = ckv.astype(BF16)
    kall_ref[:, KV_LORA:] = kpe.astype(BF16)
    hd = hd_ref[...]
    dk_ref[...] = hd[:, :DIFF_V]
    dv_ref[...] = hd[:, DIFF_V:]
    dkv_ref[:, :2 * DIFF_V] = hd.astype(BF16)
    dkv_ref[:, 2 * DIFF_V:] = jnp.ones((hd.shape[0], DKV_WIDTH - 2 * DIFF_V), BF16)


def _kv_post(h2, g_kv, cos2, sin2, q_lora, pe_off, tm):
    t = h2.shape[0]
    row = lambda i: (i, 0)
    return pl.pallas_call(
        _kv_post_kernel,
        grid=(t // tm,),
        in_specs=[pl.BlockSpec((tm, KV_LORA), lambda i: (i, q_lora // KV_LORA)),
                  pl.BlockSpec((tm, 2 * DIFF_V), lambda i: (i, (q_lora + KV_LORA) // (2 * DIFF_V))),
                  pl.BlockSpec((tm, LANE), lambda i: (i, pe_off // LANE)),
                  pl.BlockSpec((1, KV_LORA), lambda i: (0, 0)),
                  pl.BlockSpec((tm, LANE), row),
                  pl.BlockSpec((tm, LANE), row)],
        out_specs=[pl.BlockSpec((tm, KV_LORA), row),
                   pl.BlockSpec((tm, QK_ROPE), row),
                   pl.BlockSpec((tm, DIFF_V), row),
                   pl.BlockSpec((tm, DIFF_V), row),
                   pl.BlockSpec((tm, QK_ALL), row),
                   pl.BlockSpec((tm, DKV_WIDTH), row)],
        out_shape=[jax.ShapeDtypeStruct((t, KV_LORA), F32),
                   jax.ShapeDtypeStruct((t, QK_ROPE), F32),
                   jax.ShapeDtypeStruct((t, DIFF_V), F32),
                   jax.ShapeDtypeStruct((t, DIFF_V), F32),
                   jax.ShapeDtypeStruct((t, QK_ALL), BF16),
                   jax.ShapeDtypeStruct((t, DKV_WIDTH), BF16)],
        compiler_params=_cparams(("parallel",)),
        name="kv_post",
    )(h2, h2, h2, g_kv, cos2, sin2)


def _q_proj_kernel(hq_ref, g_ref, wuq_ref, wk_ref, cos_ref, sin_ref, o_ref, *, n_heads):
    hq = hq_ref[...]
    qn = hq * lax.rsqrt(jnp.mean(hq * hq, axis=-1, keepdims=True) + RMS_EPS) * g_ref[...]
    q = jnp.dot(qn.astype(BF16), wuq_ref[...], preferred_element_type=F32)
    for h in range(n_heads):
        qh = q[:, h * QK_NOPE:(h + 1) * QK_NOPE].astype(BF16)
        o_ref[h, :, :KV_LORA] = jnp.dot(qh, wk_ref[h], preferred_element_type=F32).astype(BF16)
    pe0 = n_heads * QK_NOPE
    cos2, sin2 = cos_ref[...], sin_ref[...]
    for g in range(n_heads // 2):
        r = _rope128(q[:, pe0 + g * LANE: pe0 + (g + 1) * LANE], cos2, sin2).astype(BF16)
        o_ref[2 * g, :, KV_LORA:] = r[:, :QK_ROPE]
        o_ref[2 * g + 1, :, KV_LORA:] = r[:, QK_ROPE:]


def _q_proj(h2, g_q, wuq, wk, cos2, sin2, q_lora, n_heads, tm):
    t = h2.shape[0]
    return pl.pallas_call(
        functools.partial(_q_proj_kernel, n_heads=n_heads),
        grid=(t // tm,),
        in_specs=[pl.BlockSpec((tm, q_lora), lambda i: (i, 0)),
                  pl.BlockSpec((1, q_lora), lambda i: (0, 0)),
                  pl.BlockSpec(wuq.shape, lambda i: (0, 0)),
                  pl.BlockSpec(wk.shape, lambda i: (0, 0, 0)),
                  pl.BlockSpec((tm, LANE), lambda i: (i, 0)),
                  pl.BlockSpec((tm, LANE), lambda i: (i, 0))],
        out_specs=pl.BlockSpec((n_heads, tm, QK_ALL), lambda i: (0, i, 0)),
        out_shape=jax.ShapeDtypeStruct((n_heads, t, QK_ALL), BF16),
        compiler_params=_cparams(("parallel",)),
        name="q_proj",
    )(h2, g_q, wuq, wk, cos2, sin2)


def _sm_step(s, v_bf, m_ref, l_ref, acc_ref):
    m_old = m_ref[...]
    m_new = jnp.maximum(m_old, jnp.max(s, axis=-1, keepdims=True))
    corr = jnp.exp(m_old - m_new)
    p = jnp.exp(s - m_new)
    l_ref[...] = l_ref[...] * corr + jnp.sum(p, axis=-1, keepdims=True)
    acc_ref[...] = acc_ref[...] * corr + jnp.dot(p.astype(BF16), v_bf, preferred_element_type=F32)
    m_ref[...] = m_new


def _sm_reset(m_ref, l_ref, acc_ref):
    m_ref[...] = jnp.full(m_ref.shape, -jnp.inf, F32)
    l_ref[...] = jnp.zeros(l_ref.shape, F32)
    acc_ref[...] = jnp.zeros(acc_ref.shape, F32)


_NT = (((1,), (1,)), ((), ()))


def _mla_prompt_kernel(q_ref, k_ref, wv_ref, o_ref, m_ref, l_ref, acc_ref, *, blk, n_grp):
    i = pl.program_id(1)
    q = q_ref[...].reshape(n_grp * blk, QK_ALL)
    _sm_reset(m_ref, l_ref, acc_ref)

    def logits(j):
        k = k_ref[pl.ds(pl.multiple_of(j * blk, blk), blk), :]
        s = lax.dot_general(q, k, _NT, preferred_element_type=F32) * MLA_SCALE
        return s, k[:, :KV_LORA]

    def body(j, c):
        s, v = logits(j)
        _sm_step(s, v, m_ref, l_ref, acc_ref)
        return c

    lax.fori_loop(0, i, body, 0)
    s, v = logits(i)
    row = lax.broadcasted_iota(jnp.int32, s.shape, 0) & (blk - 1)
    col = lax.broadcasted_iota(jnp.int32, s.shape, 1)
    _sm_step(jnp.where(row >= col, s, -jnp.inf), v, m_ref, l_ref, acc_ref)
    o_lat = (acc_ref[...] / l_ref[...]).astype(BF16)
    for g in range(n_grp):
        o_ref[:, g * V_HEAD:(g + 1) * V_HEAD] = jnp.dot(
            o_lat[g * blk:(g + 1) * blk], wv_ref[g], preferred_element_type=F32).astype(o_ref.dtype)


def _mla_prompt(q_all, kall, wv, batch, seq, blk, n_grp):
    n_heads = q_all.shape[0]
    nq = seq // blk
    rows = n_grp * blk
    return pl.pallas_call(
        functools.partial(_mla_prompt_kernel, blk=blk, n_grp=n_grp),
        grid=(batch, nq, n_heads // n_grp),
        in_specs=[pl.BlockSpec((n_grp, blk, QK_ALL), lambda b, i, h: (h, b * nq + i, 0)),
                  pl.BlockSpec((seq, QK_ALL), lambda b, i, h: (b, 0)),
                  pl.BlockSpec((n_grp, KV_LORA, V_HEAD), lambda b, i, h: (h, 0, 0))],
        out_specs=pl.BlockSpec((blk, n_grp * V_HEAD), lambda b, i, h: (b * nq + i, h)),
        out_shape=jax.ShapeDtypeStruct((batch * seq, n_heads * V_HEAD), BF16),
        scratch_shapes=[pltpu.VMEM((rows, 1), F32), pltpu.VMEM((rows, 1), F32),
                        pltpu.VMEM((rows, KV_LORA), F32)],
        compiler_params=_cparams(("parallel", "parallel", "arbitrary")),
        name="mla_prompt",
    )(q_all, kall, wv)


def _diff_lambda(dl, lam_init):
    s1 = jnp.sum(dl[0:1, :] * dl[1:2, :], axis=-1, keepdims=True)
    s2 = jnp.sum(dl[2:3, :] * dl[3:4, :], axis=-1, keepdims=True)
    return jnp.exp(s1) - jnp.exp(s2) + lam_init


def _diff_merge(o1, o2, lam, g_sub, lam_init):
    d = o1 - lam * o2
    d = d * lax.rsqrt(jnp.mean(d * d, axis=-1, keepdims=True) + SUBLN_EPS) * g_sub
    return d * (1.0 - lam_init)


def _diff_prompt_kernel(q1_ref, q2_ref, kv_ref, bias_ref, far_ref, dl_ref, g_ref, o_ref,
                        m1, a1, m2, a2, *, blk, n_grp, lam_init):
    i = pl.program_id(1)

    def stack(ref):
        x = ref[...].astype(BF16)
        return jnp.concatenate([x[:, g * DIFF_HEAD:(g + 1) * DIFF_HEAD] for g in range(n_grp)], axis=0)

    q1 = stack(q1_ref)
    q2 = stack(q2_ref)
    for m_ref, a_ref in ((m1, a1), (m2, a2)):
        m_ref[...] = jnp.full(m_ref.shape, -jnp.inf, F32)
        a_ref[...] = jnp.zeros(a_ref.shape, F32)

    def step(s, shift, v_aug, m_ref, a_ref):
        m_old = m_ref[...]
        top = jnp.max(s, axis=-1, keepdims=True)
        m_new = jnp.maximum(m_old, top if shift is None else top + shift)
        p = jnp.exp(s - (m_new if shift is None else m_new - shift))
        a_ref[...] = a_ref[...] * jnp.exp(m_old - m_new) + jnp.dot(
            p.astype(BF16), v_aug, preferred_element_type=F32)
        m_ref[...] = m_new

    def logits(j):
        kv = kv_ref[pl.ds(pl.multiple_of(j * blk, blk), blk), :]
        s1 = lax.dot_general(q1, kv[:, :DIFF_HEAD], _NT, preferred_element_type=F32) * DIFF_SCALE
        s2 = lax.dot_general(q2, kv[:, DIFF_HEAD:DIFF_V], _NT, preferred_element_type=F32) * DIFF_SCALE
        return s1, s2, kv[:, DIFF_V:]

    def far_body(j, c):
        s1, s2, v = logits(j)
        far = far_ref[...].reshape(n_grp * blk, LANE)[:, 0:1]
        step(s1, far, v, m1, a1)
        step(s2, far, v, m2, a2)
        return c

    lax.fori_loop(0, i - 1, far_body, 0)

    @pl.when(i >= 1)
    def _():
        s1, s2, v = logits(i - 1)
        b = bias_ref[1].reshape(n_grp * blk, blk)
        step(s1 + b, None, v, m1, a1)
        step(s2 + b, None, v, m2, a2)

    s1, s2, v = logits(i)
    row = lax.broadcasted_iota(jnp.int32, s1.shape, 0) & (blk - 1)
    col = lax.broadcasted_iota(jnp.int32, s1.shape, 1)
    keep = row >= col
    b = bias_ref[0].reshape(n_grp * blk, blk)
    step(jnp.where(keep, s1 + b, -jnp.inf), None, v, m1, a1)
    step(jnp.where(keep, s2 + b, -jnp.inf), None, v, m2, a2)

    def normalised(a_ref):
        a = a_ref[...]
        den = a[:, DIFF_V:]
        return a[:, :DIFF_V] / jnp.concatenate([den] * (DIFF_V // LANE), axis=1)

    lam = _diff_lambda(dl_ref[...], lam_init)
    o = _diff_merge(normalised(a1), normalised(a2), lam, g_ref[...], lam_init).astype(o_ref.dtype)
    for g in range(n_grp):
        o_ref[:, g * DIFF_V:(g + 1) * DIFF_V] = o[g * blk:(g + 1) * blk]


def _diff_prompt(h2, dkv, bias_tiles, far_tiles, dl, g_sub, batch, seq, blk, qd_off, n_dh, n_grp, lam_init):
    nq = seq // blk
    width = n_grp * DIFF_HEAD
    assert qd_off % width == 0 and (n_dh * DIFF_HEAD) % width == 0
    c1 = qd_off // width
    c2 = (qd_off + n_dh * DIFF_HEAD) // width
    rows = n_grp * blk
    return pl.pallas_call(
        functools.partial(_diff_prompt_kernel, blk=blk, n_grp=n_grp, lam_init=lam_init),
        grid=(batch, nq, n_dh // n_grp),
        in_specs=[pl.BlockSpec((blk, width), lambda b, i, h: (b * nq + i, c1 + h)),
                  pl.BlockSpec((blk, width), lambda b, i, h: (b * nq + i, c2 + h)),
                  pl.BlockSpec((seq, DKV_WIDTH), lambda b, i, h: (b, 0)),
                  pl.BlockSpec((2, n_grp, blk, blk), lambda b, i, h: (0, h, 0, 0)),
                  pl.BlockSpec((n_grp, blk, LANE), lambda b, i, h: (h, 0, 0)),
                  pl.BlockSpec((4, DIFF_HEAD), lambda b, i, h: (0, 0)),
                  pl.BlockSpec((1, DIFF_V), lambda b, i, h: (0, 0))],
        out_specs=pl.BlockSpec((blk, n_grp * DIFF_V), lambda b, i, h: (b * nq + i, h)),
        out_shape=jax.ShapeDtypeStruct((batch * seq, n_dh * DIFF_V), BF16),
        scratch_shapes=[pltpu.VMEM((rows, 1), F32), pltpu.VMEM((rows, DIFF_V + LANE), F32),
                        pltpu.VMEM((rows, 1), F32), pltpu.VMEM((rows, DIFF_V + LANE), F32)],
        compiler_params=_cparams(("parallel", "parallel", "arbitrary")),
        name="diff_prompt",
    )(h2, h2, dkv, bias_tiles, far_tiles, dl, g_sub)


def _decode_kernel(pt_ref, qm_ref, qd_ref, ks_ref, dks_ref, bfar_ref, blast_ref, bself_ref,
                   ckv_hbm, kpe_hbm, dk_hbm, dv_hbm,
                   olat_ref, od_ref,
                   ckv_buf, kpe_buf, dk_buf, dv_buf, sems,
                   mm, lm, am, md, ld, ad, *, layer, n_groups, n_batch, gp, page, n_dh):
    b = pl.program_id(0)
    g = pl.program_id(1)
    step = b * n_groups + g
    slot = step % 2

    def copies(bb, gg, sl):
        out = []
        for j in range(gp):
            phys = pt_ref[bb, gg * gp + j]
            out.append(pltpu.make_async_copy(ckv_hbm.at[layer, phys], ckv_buf.at[sl, j], sems.at[sl, 0]))
            out.append(pltpu.make_async_copy(kpe_hbm.at[layer, phys],
                                             kpe_buf.at[sl, :, pl.ds(j * page, page)], sems.at[sl, 1]))
            out.append(pltpu.make_async_copy(dk_hbm.at[layer, phys, :, 0, :],
                                             dk_buf.at[sl, j, :, pl.ds(0, DIFF_HEAD)], sems.at[sl, 2]))
            out.append(pltpu.make_async_copy(dk_hbm.at[layer, phys, :, 1, :],
                                             dk_buf.at[sl, j, :, pl.ds(DIFF_HEAD, DIFF_HEAD)], sems.at[sl, 2]))
            out.append(pltpu.make_async_copy(dv_hbm.at[layer, phys], dv_buf.at[sl, j], sems.at[sl, 3]))
        return out

    @pl.when(step == 0)
    def _():
        for c in copies(0, 0, 0):
            c.start()

    nxt = step + 1

    @pl.when(nxt < n_batch * n_groups)
    def _():
        for c in copies(nxt // n_groups, nxt % n_groups, nxt % 2):
            c.start()

    @pl.when(g == 0)
    def _():
        _sm_reset(mm, lm, am)
        _sm_reset(md, ld, ad)

    for c in copies(b, g, slot):
        c.wait()

    nk = gp * page
    qm = qm_ref[...]
    qd = qd_ref[...]
    kc = ckv_buf[slot].reshape(nk, KV_LORA).astype(BF16)
    kp_t = kpe_buf[slot].astype(BF16)
    s = lax.dot_general(qm[:, :KV_LORA], kc, _NT, preferred_element_type=F32)
    s += jnp.dot(qm[:, KV_LORA:], kp_t, preferred_element_type=F32)
    _sm_step(s * MLA_SCALE, kc, mm, lm, am)

    dk = dk_buf[slot].reshape(nk, DIFF_V).astype(BF16)
    dv = dv_buf[slot].reshape(nk, DIFF_V).astype(BF16)
    s1 = lax.dot_general(qd[:n_dh], dk[:, :DIFF_HEAD], _NT, preferred_element_type=F32)
    s2 = lax.dot_general(qd[n_dh:], dk[:, DIFF_HEAD:], _NT, preferred_element_type=F32)
    bias = jnp.where(g == n_groups - 1, blast_ref[...], bfar_ref[...])
    sd = jnp.concatenate([s1, s2], axis=0) * DIFF_SCALE + bias
    _sm_step(sd, dv, md, ld, ad)

    @pl.when(g == n_groups - 1)
    def _():
        ks = ks_ref[...].astype(F32)
        s_self = jnp.sum(qm.astype(F32) * ks, axis=-1, keepdims=True) * MLA_SCALE
        m_old = mm[...]
        m_new = jnp.maximum(m_old, s_self)
        corr = jnp.exp(m_old - m_new)
        p = jnp.exp(s_self - m_new)
        l_new = lm[...] * corr + p
        olat_ref[...] = (am[...] * corr + p * ks[:, :KV_LORA]) / l_new

        dks = dks_ref[...].astype(F32)
        qdf = qd.astype(F32)
        t1 = jnp.sum(qdf[:n_dh] * dks[:, :DIFF_HEAD], axis=-1, keepdims=True)
        t2 = jnp.sum(qdf[n_dh:] * dks[:, DIFF_HEAD:DIFF_V], axis=-1, keepdims=True)
        sd_self = jnp.concatenate([t1, t2], axis=0) * DIFF_SCALE + bself_ref[:, 0:1]
        m_old = md[...]
        m_new = jnp.maximum(m_old, sd_self)
        corr = jnp.exp(m_old - m_new)
        p = jnp.exp(sd_self - m_new)
        l_new = ld[...] * corr + p
        od_ref[...] = (ad[...] * corr + p * dks[:, DIFF_V:]) / l_new


def _decode(page_table, qm, qd, kall_s, dkv_s, bfar, blast, bself,
            cache_ckv, cache_kpe, cache_dk, cache_dv, layer, gp):
    n_batch, n_heads, _ = qm.shape
    n_maps = qd.shape[1]
    n_pages = page_table.shape[1]
    page = cache_ckv.shape[2]
    n_groups = n_pages // gp
    nk = gp * page
    per_b = lambda b, g, pt: (b, 0, 0)
    const2 = lambda b, g, pt: (0, 0)
    any_spec = pl.BlockSpec(memory_space=pl.ANY)
    grid_spec = pltpu.PrefetchScalarGridSpec(
        num_scalar_prefetch=1,
        grid=(n_batch, n_groups),
        in_specs=[pl.BlockSpec((None, n_heads, QK_ALL), per_b),
                  pl.BlockSpec((None, n_maps, DIFF_HEAD), per_b),
                  pl.BlockSpec((None, 1, QK_ALL), per_b),
                  pl.BlockSpec((None, 1, 2 * DIFF_V), per_b),
                  pl.BlockSpec((n_maps, nk), const2),
                  pl.BlockSpec((n_maps, nk), const2),
                  pl.BlockSpec((n_maps, LANE), const2),
                  any_spec, any_spec, any_spec, any_spec],
        out_specs=[pl.BlockSpec((None, n_heads, KV_LORA), per_b),
                   pl.BlockSpec((None, n_maps, DIFF_V), per_b)],
        scratch_shapes=[pltpu.VMEM((2, gp, page, KV_LORA), F32),
                        pltpu.VMEM((2, QK_ROPE, gp * page), F32),
                        pltpu.VMEM((2, gp, page, DIFF_V), F32),
                        pltpu.VMEM((2, gp, page, DIFF_V), F32),
                        pltpu.SemaphoreType.DMA((2, 4)),
                        pltpu.VMEM((n_heads, 1), F32), pltpu.VMEM((n_heads, 1), F32),
                        pltpu.VMEM((n_heads, KV_LORA), F32),
                        pltpu.VMEM((n_maps, 1), F32), pltpu.VMEM((n_maps, 1), F32),
                        pltpu.VMEM((n_maps, DIFF_V), F32)])
    return pl.pallas_call(
        functools.partial(_decode_kernel, layer=layer, n_groups=n_groups, n_batch=n_batch, gp=gp,
                          page=page, n_dh=n_maps // 2),
        grid_spec=grid_spec,
        out_shape=[jax.ShapeDtypeStruct((n_batch, n_heads, KV_LORA), F32),
                   jax.ShapeDtypeStruct((n_batch, n_maps, DIFF_V), F32)],
        compiler_params=_cparams(("arbitrary", "arbitrary")),
        name="decode_attn",
    )(page_table, qm, qd, kall_s, dkv_s, bfar, blast, bself, cache_ckv, cache_kpe, cache_dk, cache_dv)


def _decode_merge_kernel(olat_ref, wv_ref, od_ref, dl_ref, g_ref, omla_ref, odiff_ref, *, n_dh, lam_init):
    h = pl.program_id(0)
    omla_ref[...] = jnp.dot(olat_ref[...].astype(BF16), wv_ref[...],
                            preferred_element_type=F32).astype(omla_ref.dtype)

    @pl.when(h < n_dh)
    def _():
        lam = _diff_lambda(dl_ref[...], lam_init)
        odiff_ref[...] = _diff_merge(od_ref[0], od_ref[1], lam, g_ref[...], lam_init).astype(odiff_ref.dtype)


def _decode_merge(olat_h, wv, od_h, dl, g_sub, lam_init):
    n_heads, n_batch, _ = olat_h.shape
    n_dh = od_h.shape[1]
    return pl.pallas_call(
        functools.partial(_decode_merge_kernel, n_dh=n_dh, lam_init=lam_init),
        grid=(n_heads,),
        in_specs=[pl.BlockSpec((None, n_batch, KV_LORA), lambda h: (h, 0, 0)),
                  pl.BlockSpec((None, KV_LORA, V_HEAD), lambda h: (h, 0, 0)),
                  pl.BlockSpec((2, None, n_batch, DIFF_V), lambda h: (0, jnp.minimum(h, n_dh - 1), 0, 0)),
                  pl.BlockSpec((4, DIFF_HEAD), lambda h: (0, 0)),
                  pl.BlockSpec((1, DIFF_V), lambda h: (0, 0))],
        out_specs=[pl.BlockSpec((n_batch, V_HEAD), lambda h: (0, h)),
                   pl.BlockSpec((n_batch, DIFF_V), lambda h: (0, jnp.minimum(h, n_dh - 1)))],
        out_shape=[jax.ShapeDtypeStruct((n_batch, n_heads * V_HEAD), BF16),
                   jax.ShapeDtypeStruct((n_batch, n_dh * DIFF_V), BF16)],
        compiler_params=_cparams(("arbitrary",)),
        name="decode_merge",
    )(olat_h, wv, od_h, dl, g_sub)


def _layernorm(z, g, b):
    mu = jnp.mean(z, axis=-1, keepdims=True)
    zc = z - mu
    var = jnp.mean(zc * zc, axis=-1, keepdims=True)
    return zc * lax.rsqrt(var + LN_EPS) * g + b


def _ln_router_kernel(z_ref, x_ref, g_ref, b_ref, wr_ref, br_ref, h_ref, hpk_ref, ids_ref, gates_ref,
                      *, alpha, n_experts):
    hmid = _layernorm(alpha * x_ref[...] + z_ref[...], g_ref[...], b_ref[...])
    h_ref[...] = hmid
    hbf = hmid.astype(BF16)
    bits = lax.bitcast_convert_type(hbf.astype(F32), jnp.uint32)
    half = bits.shape[1] // 2
    hpk_ref[...] = (bits[:, :half] >> 16) | (bits[:, half:] & jnp.uint32(0xFFFF0000))
    logits = jnp.dot(hbf, wr_ref[...], preferred_element_type=F32) + br_ref[...]
    lane = lax.broadcasted_iota(jnp.int32, logits.shape, 1)
    out_lane = lax.broadcasted_iota(jnp.int32, ids_ref.shape, 1)
    ids = jnp.zeros(ids_ref.shape, jnp.int32)
    vals = jnp.zeros(gates_ref.shape, F32)
    top = None
    denom = None
    for k in range(TOP_K):
        v = jnp.max(logits, axis=-1, keepdims=True)
        idx = jnp.min(jnp.where(logits == v, lane, n_experts), axis=-1, keepdims=True)
        logits = jnp.where(lane == idx, -jnp.inf, logits)
        if k == 0:
            top = v
        e = jnp.exp(v - top)
        denom = e if k == 0 else denom + e
        ids = jnp.where(out_lane == k, idx, ids)
        vals = jnp.where(out_lane == k, e, vals)
    ids_ref[...] = ids
    gates_ref[...] = vals / denom


def _ln_router(z, x, g, b, wr, br, alpha, tm):
    t, d = z.shape
    n_experts = wr.shape[1]
    row = lambda i: (i, 0)
    c = lambda i: (0, 0)
    return pl.pallas_call(
        functools.partial(_ln_router_kernel, alpha=alpha, n_experts=n_experts),
        grid=(t // tm,),
        in_specs=[pl.BlockSpec((tm, d), row), pl.BlockSpec((tm, d), row),
                  pl.BlockSpec((1, d), c), pl.BlockSpec((1, d), c),
                  pl.BlockSpec((d, n_experts), c), pl.BlockSpec((1, n_experts), c)],
        out_specs=[pl.BlockSpec((tm, d), row), pl.BlockSpec((tm, d // 2), row),
                   pl.BlockSpec((tm, LANE), row), pl.BlockSpec((tm, LANE), row)],
        out_shape=[jax.ShapeDtypeStruct((t, d), F32), jax.ShapeDtypeStruct((t, d // 2), jnp.uint32),
                   jax.ShapeDtypeStruct((t, LANE), jnp.int32), jax.ShapeDtypeStruct((t, LANE), F32)],
        compiler_params=_cparams(("parallel",)),
        name="ln_router",
    )(z, x, g, b, wr, br)


def _moe_kernel(ex_ref, r0_ref, ns_ref, tok_ref, used_ref,
                hpk_hbm, perm_ref, wgu_ref, bgu_ref, wd_ref, bd_ref, ys_hbm,
                xraw, act, wbf, ybuf, xsem, ysem, *, n_ff, n_down, n_items, chunk):
    w = pl.program_id(0)
    s = pl.program_id(1)
    nsub = ns_ref[w]
    row0 = r0_ref[w]
    half = xraw.shape[1]
    nxt = jnp.minimum(w + 1, n_items - 1)
    chunks_per_sub = ROW_SUB // chunk

    def issue_chunk(base, k):
        for u in range(chunk):
            r = k * chunk + u
            pltpu.make_async_copy(hpk_hbm.at[pl.ds(tok_ref[base + r], 1)], xraw.at[pl.ds(r, 1)],
                                  xsem.at[0]).start()

    def issue_chunks(base, lo, hi):
        def go(k, c):
            issue_chunk(base, k)
            return c
        lax.fori_loop(lo, hi, go, 0)

    def wait_chunks(n):
        def go(k, c):
            pltpu.make_async_copy(hpk_hbm.at[pl.ds(0, chunk)], xraw.at[pl.ds(0, chunk)], xsem.at[0]).wait()
            return c
        lax.fori_loop(0, n, go, 0)

    @pl.when(s == 0)
    def _():
        @pl.when(w == 0)
        def _():
            issue_chunks(row0, 0, nsub * chunks_per_sub)
            wait_chunks(nsub * chunks_per_sub)

        @pl.when(w > 0)
        def _():
            wait_chunks(jnp.maximum(ns_ref[jnp.maximum(w - 1, 0)], nsub) * chunks_per_sub)

    @pl.when(s < n_ff)
    def _():
        wbf[...] = wgu_ref[...].astype(BF16)
        bias = bgu_ref[...]
        perm = perm_ref[...]
        lane = lax.broadcasted_iota(jnp.int32, (ROW_SUB, LANE), 1)
        even = (lane & 1) == 0

        def sub(r):
            rows = pl.ds(pl.multiple_of(r * ROW_SUB, ROW_SUB), ROW_SUB)
            xr = xraw[rows, :]
            x_lo = lax.bitcast_convert_type(xr << 16, F32).astype(BF16)
            x_hi = lax.bitcast_convert_type(xr & jnp.uint32(0xFFFF0000), F32).astype(BF16)
            hgu = jnp.dot(x_lo, wbf[pl.ds(0, half), :], preferred_element_type=F32)
            hgu = hgu + jnp.dot(x_hi, wbf[pl.ds(half, half), :], preferred_element_type=F32) + bias
            for q in range(FF_TILE // LANE):
                a = hgu[:, (2 * q) * LANE:(2 * q + 1) * LANE]
                bq = hgu[:, (2 * q + 1) * LANE:(2 * q + 2) * LANE]
                gate = jnp.where(even, a, pltpu.roll(bq, 1, 1))
                up = jnp.where(even, pltpu.roll(a, LANE - 1, 1), bq)
                gate = jnp.minimum(gate, SWIGLU_LIMIT)
                up = jnp.clip(up, -SWIGLU_LIMIT, SWIGLU_LIMIT)
                mixed = ((up + 1.0) * gate * jax.nn.sigmoid(SWIGLU_ALPHA * gate)).astype(BF16)
                act[s, rows, q * LANE:(q + 1) * LANE] = jnp.dot(
                    mixed, perm, preferred_element_type=F32).astype(BF16)

        def pair(i, c):
            sub(2 * i)
            sub(2 * i + 1)
            return c
        lax.fori_loop(0, nsub // 2, pair, 0)

        @pl.when(nsub % 2 == 1)
        def _():
            sub(nsub - 1)

    def y_copy(first_row, n, sl):
        dst = ys_hbm.at[pl.ds(pl.multiple_of(first_row, ROW_SUB), ROW_SUB),
                        pl.ds(pl.multiple_of(n * DOWN_TILE, DOWN_TILE), DOWN_TILE)]
        return pltpu.make_async_copy(ybuf.at[sl], dst, ysem.at[sl])

    @pl.when(s >= n_ff)
    def _():
        n = s - n_ff
        wd_bf = wbf.at[pl.ds(0, n_ff * FF_TILE)]
        wd_bf[...] = wd_ref[...].astype(BF16)
        bias = bd_ref[...]
        nxt_row0 = r0_ref[nxt]

        def sub(r, sl):
            rows = pl.ds(pl.multiple_of(r * ROW_SUB, ROW_SUB), ROW_SUB)
            y = bias
            for j in range(n_ff):
                y = y + jnp.dot(act[j, rows, :], wd_bf[pl.ds(j * FF_TILE, FF_TILE), :],
                                preferred_element_type=F32)
            ybuf[sl] = y
            y_copy(row0 + r * ROW_SUB, n, sl).start()

        def pair(i, reuse):
            issue_chunk(nxt_row0, n * nsub + 2 * i)
            issue_chunk(nxt_row0, n * nsub + 2 * i + 1)
            if reuse:
                y_copy(row0, n, 0).wait()
                y_copy(row0, n, 1).wait()
            sub(2 * i, 0)
            sub(2 * i + 1, 1)

        @pl.when(nsub >= 2)
        def _():
            pair(0, False)

        def later_pair(i, c):
            pair(i, True)
            return c
        lax.fori_loop(1, nsub // 2, later_pair, 0)

        @pl.when(nsub % 2 == 1)
        def _():
            issue_chunk(nxt_row0, n * nsub + nsub - 1)

            @pl.when(nsub >= 3)
            def _():
                y_copy(row0, n, 0).wait()
            sub(nsub - 1, 0)

        @pl.when(nsub >= 2)
        def _():
            y_copy(row0, n, nsub % 2).wait()

        @pl.when(nsub >= 1)
        def _():
            y_copy(row0, n, (nsub - 1) % 2).wait()

        @pl.when(s == n_ff + n_down - 1)
        def _():
            issue_chunks(nxt_row0, nsub * n_down, ns_ref[nxt] * n_down)

    @pl.when((w == n_items - 1) & (s == n_ff + n_down - 1))
    def _():
        ybuf[0] = jnp.zeros((ROW_SUB, DOWN_TILE), F32)
        first = used_ref[0]
        total = ys_hbm.shape[0] // ROW_SUB

        def fill(tile, c):
            for n in range(n_down):
                y_copy(tile * ROW_SUB, n, 0).start()
            return c
        lax.fori_loop(first, total, fill, 0)

        def drain(tile, c):
            for n in range(n_down):
                y_copy(tile * ROW_SUB, n, 0).wait()
            return c
        lax.fori_loop(first, total, drain, 0)


def _moe_ffn(item_expert, item_row0, item_nsub, tok_of_row, n_used, hpk, perm,
             w_gate_up, b_gate_up, w_down, b_down, layer, n_rows):
    d_model = 2 * hpk.shape[1]
    n_experts, _, two_de = w_gate_up.shape[1:]
    d_expert = two_de // 2
    n_ff = d_expert // FF_TILE
    n_down = d_model // DOWN_TILE
    n_items = item_expert.shape[0]
    chunk = ROW_SUB // n_down
    bgu = b_gate_up.reshape(b_gate_up.shape[0], n_experts, 1, two_de)
    bd = b_down.reshape(b_down.shape[0], n_experts, 1, d_model)

    def wgu_map(w, s, ex, r0, ns, tok, used):
        return (layer, ex[w], 0, jnp.where(ns[w] > 0, jnp.minimum(s, n_ff - 1), n_ff - 1))

    def wd_map(w, s, ex, r0, ns, tok, used):
        return (layer, ex[w], 0, jnp.where(ns[w] > 0, jnp.maximum(s - n_ff, 0), n_down - 1))

    grid_spec = pltpu.PrefetchScalarGridSpec(
        num_scalar_prefetch=5,
        grid=(n_items, n_ff + n_down),
        in_specs=[pl.BlockSpec(memory_space=pl.ANY),
                  pl.BlockSpec((LANE, LANE), lambda w, s, *_: (0, 0)),
                  pl.BlockSpec((None, None, d_model, 2 * FF_TILE), wgu_map),
                  pl.BlockSpec((None, None, 1, 2 * FF_TILE), wgu_map),
                  pl.BlockSpec((None, None, d_expert, DOWN_TILE), wd_map),
                  pl.BlockSpec((None, None, 1, DOWN_TILE), wd_map)],
        out_specs=pl.BlockSpec(memory_space=pl.ANY),
        scratch_shapes=[pltpu.VMEM((ROWS_PER_ITEM, d_model // 2), jnp.uint32),
                        pltpu.VMEM((n_ff, ROWS_PER_ITEM, FF_TILE), BF16),
                        pltpu.VMEM((max(d_model, d_expert), 2 * FF_TILE), BF16),
                        pltpu.VMEM((2, ROW_SUB, DOWN_TILE), F32),
                        pltpu.SemaphoreType.DMA((1,)),
                        pltpu.SemaphoreType.DMA((2,))])
    return pl.pallas_call(
        functools.partial(_moe_kernel, n_ff=n_ff, n_down=n_down, n_items=n_items, chunk=chunk),
        grid_spec=grid_spec,
        out_shape=jax.ShapeDtypeStruct((n_rows, d_model), F32),
        compiler_params=_cparams(("arbitrary", "arbitrary")),
        name="moe_ffn",
    )(item_expert, item_row0, item_nsub, tok_of_row, n_used, hpk, perm, w_gate_up, bgu, w_down, bd)


def _combine_ln_kernel(slot_ref, ys_hbm, gates_ref, h_ref, g_ref, b_ref, op_ref, os_ref, ybuf, sem,
                       *, alpha, n_prompt_tiles, n_tiles, tm):
    i = pl.program_id(0)
    cur = i % 2

    def issue(tile, buf):
        def go(r, c):
            for k in range(TOP_K):
                src = ys_hbm.at[pl.ds(slot_ref[(tile * tm + r) * TOP_K + k], 1)]
                pltpu.make_async_copy(src, ybuf.at[buf, k, pl.ds(r, 1)], sem.at[buf]).start()
            return c
        lax.fori_loop(0, tm, go, 0)

    @pl.when(i == 0)
    def _():
        issue(0, 0)

    @pl.when(i + 1 < n_tiles)
    def _():
        issue(i + 1, (i + 1) % 2)

    for k in range(TOP_K):
        pltpu.make_async_copy(ys_hbm.at[pl.ds(0, tm)], ybuf.at[cur, k], sem.at[cur]).wait()

    gates = gates_ref[...]
    ffn = gates[:, 0:1] * ybuf[cur, 0]
    for k in range(1, TOP_K):
        ffn = ffn + gates[:, k:k + 1] * ybuf[cur, k]
    out = _layernorm(alpha * h_ref[...] + ffn, g_ref[...], b_ref[...])

    @pl.when(i < n_prompt_tiles)
    def _():
        op_ref[...] = out

    @pl.when(i >= n_prompt_tiles)
    def _():
        os_ref[...] = out


def _combine_ln(slot_flat, ys, gates, hmid, g, b, alpha, tm, tp):
    t, d = hmid.shape
    assert tp % tm == 0 and (t - tp) % tm == 0
    npt = tp // tm
    n_tiles = t // tm
    row = lambda i, sl: (i, 0)
    c = lambda i, sl: (0, 0)
    grid_spec = pltpu.PrefetchScalarGridSpec(
        num_scalar_prefetch=1,
        grid=(n_tiles,),
        in_specs=[pl.BlockSpec(memory_space=pl.ANY),
                  pl.BlockSpec((tm, LANE), row), pl.BlockSpec((tm, d), row),
                  pl.BlockSpec((1, d), c), pl.BlockSpec((1, d), c)],
        out_specs=[pl.BlockSpec((tm, d), lambda i, sl: (jnp.minimum(i, npt - 1), 0)),
                   pl.BlockSpec((tm, d), lambda i, sl: (jnp.maximum(i - npt, 0), 0))],
        scratch_shapes=[pltpu.VMEM((2, TOP_K, tm, d), F32), pltpu.SemaphoreType.DMA((2,))])
    return pl.pallas_call(
        functools.partial(_combine_ln_kernel, alpha=alpha, n_prompt_tiles=npt, n_tiles=n_tiles, tm=tm),
        grid_spec=grid_spec,
        out_shape=[jax.ShapeDtypeStruct((tp, d), F32), jax.ShapeDtypeStruct((t - tp, d), F32)],
        compiler_params=_cparams(("arbitrary",)),
        name="combine_ln",
    )(slot_flat, ys, gates, hmid, g, b)


def _rope_tables(pos):
    inv = ROPE_THETA ** (-jnp.arange(0, QK_ROPE, 2, dtype=F32) / QK_ROPE)
    ang = pos.astype(F32)[:, None] * inv[None, :]
    cos, sin = jnp.cos(ang), jnp.sin(ang)
    return jnp.tile(cos, (1, 4)), jnp.tile(jnp.concatenate([-sin, sin], axis=-1), (1, 2))


def _t5_bucket(dist):
    n = jnp.maximum(dist, 0)
    max_exact = N_BUCKETS // 2
    large = max_exact + (jnp.log(jnp.maximum(n, max_exact).astype(F32) / max_exact)
                         / math.log(MAX_DISTANCE / max_exact) * (N_BUCKETS - max_exact)).astype(jnp.int32)
    return jnp.where(n < max_exact, n, jnp.minimum(large, N_BUCKETS - 1))


def _bias_tiles(rb, blk):
    n_dh = rb.shape[1]
    period = 2 * blk
    tiles = []
    for tt in range(2):
        e = jnp.arange(period, dtype=jnp.int32)
        e = jnp.where(e < blk, e, e - period)
        v = rb[_t5_bucket(tt * blk - e)].T
        x = jnp.tile(v, (1, blk))[:, :blk * (period - 1)].reshape(n_dh, blk, period - 1)
        tiles.append(x[:, :, :blk])
    return jnp.stack(tiles)


def _route_layout(ids, n_experts, n_items, n_rows):
    t = ids.shape[0]
    flat = ids.reshape(-1)
    onehot = (flat[:, None] == jnp.arange(n_experts, dtype=jnp.int32)[None, :]).astype(jnp.int32)
    csum = jnp.cumsum(onehot, axis=0)
    counts = csum[-1]
    rank = jnp.take_along_axis(csum, flat[:, None], axis=1)[:, 0] - 1
    nsub_e = (counts + ROW_SUB - 1) // ROW_SUB
    seg_start = (jnp.cumsum(nsub_e) - nsub_e) * ROW_SUB
    slot = seg_start[flat] + rank
    tok_of_row = jnp.zeros((n_rows + ROWS_PER_ITEM,), jnp.int32).at[slot].set(
        jnp.arange(t * TOP_K, dtype=jnp.int32) // TOP_K, unique_indices=True)
    n_used = jnp.sum(nsub_e).astype(jnp.int32).reshape(1)
    sub_per_item = ROWS_PER_ITEM // ROW_SUB
    assert n_items * sub_per_item > n_rows // ROW_SUB + (sub_per_item - 1) * n_experts, "last item must be empty"
    items_e = (nsub_e + sub_per_item - 1) // sub_per_item
    item_end = jnp.cumsum(items_e)
    item_start = item_end - items_e
    w = jnp.arange(n_items, dtype=jnp.int32)
    n_valid = item_end[-1]
    e_of_w = jnp.minimum(jnp.searchsorted(item_end, w, side="right"), n_experts - 1).astype(jnp.int32)
    chunk = w - item_start[e_of_w]
    nsub_w = jnp.clip(nsub_e[e_of_w] - chunk * sub_per_item, 0, sub_per_item)
    valid = w < n_valid
    last_e = e_of_w[jnp.maximum(n_valid - 1, 0)]
    item_expert = jnp.where(valid, e_of_w, last_e).astype(jnp.int32)
    item_nsub = jnp.where(valid, nsub_w, 0).astype(jnp.int32)
    item_row0 = jnp.where(valid, seg_start[e_of_w] + chunk * ROWS_PER_ITEM, 0).astype(jnp.int32)
    return slot.reshape(t, TOP_K), tok_of_row, n_used, item_expert, item_row0, item_nsub


def _act_perm():
    n = np.arange(LANE)
    src = np.where(n < LANE // 2, 2 * n, 2 * (n - LANE // 2) + 1)
    p = np.zeros((LANE, LANE), np.float32)
    p[src, n] = 1.0
    return jnp.asarray(p, BF16)


def kernel(x_prompt, x_sample, cache_ckv, cache_kpe, cache_dk, cache_dv, page_table, rel_bias,
           w_in, g_q, w_uq, g_kv, w_ukv, diff_lambda, g_sub, w_out, ln1_g, ln1_b,
           w_router, b_router, w_gate_up, b_gate_up, w_down, b_down, ln2_g, ln2_b):
    bp, sp, d = x_prompt.shape
    bs, ss, _ = x_sample.shape
    assert ss == 1, "decode path handles one new token per sequence"
    depth = w_in.shape[0]
    n_heads = (d // 2) // V_HEAD
    n_dh = (d // 2) // DIFF_V
    q_lora = w_uq.shape[1]
    n_experts = w_router.shape[2]
    page = cache_ckv.shape[2]
    n_pages = page_table.shape[1]
    past_len = n_pages * page
    tp, ts = bp * sp, bs * ss
    t = tp + ts
    dn_alpha = (2 * depth) ** 0.25

    c_ckv = q_lora
    c_dkv = c_ckv + KV_LORA
    c_qd = c_dkv + 2 * DIFF_V
    c_pe = c_qd + 2 * n_dh * DIFF_HEAD
    n_in = c_pe + LANE
    assert c_ckv % KV_LORA == 0 and c_dkv % (2 * DIFF_V) == 0
    assert (t * TOP_K) % ROW_SUB == 0
    i1 = q_lora
    i2 = i1 + KV_LORA + QK_ROPE
    i3 = i2 + 2 * n_dh * DIFF_HEAD
    i4 = i3 + 2 * DIFF_HEAD

    tm_tok = _pick(t, (640, 320, 256, 128, 64, 32, 16, 8))
    tm_big = _pick(t, (1664, 640, 320, 256, 128, 64, 32, 16, 8))
    blk = _pick(sp, (256, 128))
    gp = _pick(n_pages, (PAGES_PER_STEP, 4, 2, 1))
    assert sp % blk == 0 and blk >= MAX_DISTANCE and page >= MAX_DISTANCE

    pos_p = jnp.arange(sp, dtype=jnp.int32)
    pos_all = jnp.concatenate([jnp.tile(pos_p, bp), jnp.full((ts,), past_len, jnp.int32)])
    cos2, sin2 = _rope_tables(pos_all)

    far_idx = _t5_bucket(jnp.asarray(2 * blk, jnp.int32))
    offs = jnp.arange(page, dtype=jnp.int32)
    last_idx = _t5_bucket(past_len - ((n_pages - 1) * page + offs))
    far_dec_idx = _t5_bucket(jnp.asarray(past_len - ((n_pages - 1) * page - 1), jnp.int32))

    x_all = jnp.concatenate([x_prompt.reshape(tp, d), x_sample.reshape(ts, d)], axis=0)
    perm = _act_perm()
    kpe_t = jnp.swapaxes(cache_kpe, 2, 3)
    new_p, new_s = [], []
    for l in range(depth):
        lam_init = 0.8 - 0.6 * math.exp(-0.3 * l)
        wi = w_in[l]
        w_in2 = jnp.concatenate(
            [wi[:, :i1], wi[:, i1:i1 + KV_LORA], wi[:, i3:i4], wi[:, i4:], wi[:, i2:i3],
             wi[:, i1 + KV_LORA:i2], jnp.zeros((d, LANE - QK_ROPE), wi.dtype)], axis=1).astype(BF16)
        wuq = w_uq[l].reshape(q_lora, n_heads, QK_NOPE + QK_ROPE)
        wuq2 = jnp.concatenate([wuq[:, :, :QK_NOPE].reshape(q_lora, -1),
                                wuq[:, :, QK_NOPE:].reshape(q_lora, -1)], axis=1).astype(BF16)
        wk = jnp.transpose(w_ukv[l][:, :, :QK_NOPE], (1, 2, 0)).astype(BF16)
        wv = jnp.transpose(w_ukv[l][:, :, QK_NOPE:], (1, 0, 2)).astype(BF16)
        wo = w_out[l].astype(BF16)
        half = n_heads * V_HEAD

        x_bf = x_all.astype(BF16)
        h2 = _matmul(x_bf, w_in2, tm_big, _pick(n_in, (384, 256, 128)), F32, "in_proj")
        ckv, kpe, dk, dv, kall, dkv = _kv_post(h2, g_kv[l][None, :], cos2, sin2, q_lora, c_pe, tm_tok)
        q_all = _q_proj(h2, g_q[l][None, :], wuq2, wk, cos2, sin2, q_lora, n_heads,
                        _pick(t, (320, 256, 128, 64, 32, 16, 8)))

        rb = rel_bias.astype(F32)
        bias_tiles = _bias_tiles(rb, blk)
        far_tiles = jnp.broadcast_to(rb[far_idx][:, None, None], (n_dh, blk, LANE))
        dl = diff_lambda[l].astype(F32)
        gs = g_sub[l][None, :].astype(F32)
        mix_mla_p = _mla_prompt(q_all, kall, wv, bp, sp, blk, _pick(n_heads, (4, 2, 1)))
        mix_diff_p = _diff_prompt(h2, dkv, bias_tiles, far_tiles, dl, gs, bp, sp, blk, c_qd, n_dh,
                                  _pick(n_dh, (4, 2, 1)), lam_init)

        qm_s = jnp.transpose(q_all[:, tp:, :], (1, 0, 2))
        qd_s = h2[tp:, c_qd:c_qd + 2 * n_dh * DIFF_HEAD].astype(BF16).reshape(ts, 2 * n_dh, DIFF_HEAD)
        nk = gp * page
        bfar = jnp.broadcast_to(jnp.tile(rb[far_dec_idx], 2)[:, None], (2 * n_dh, nk))
        blast = jnp.concatenate([bfar[:, :nk - page], jnp.tile(rb[last_idx].T, (2, 1))], axis=1)
        bself = jnp.broadcast_to(jnp.tile(rb[0], 2)[:, None], (2 * n_dh, LANE))
        olat, od = _decode(page_table, qm_s, qd_s, kall[tp:, None, :], dkv[tp:, None, :],
                           bfar, blast, bself, cache_ckv, kpe_t, cache_dk, cache_dv, l, gp)
        od_h = jnp.transpose(od.reshape(ts, 2, n_dh, DIFF_V), (1, 2, 0, 3))
        mix_mla_s, mix_diff_s = _decode_merge(jnp.transpose(olat, (1, 0, 2)), wv, od_h, dl, gs, lam_init)

        mix_a = jnp.concatenate([mix_mla_p, mix_mla_s], axis=0)
        mix_b = jnp.concatenate([mix_diff_p, mix_diff_s], axis=0)
        z = _matmul2(mix_a, mix_b, wo[:half], wo[half:], tm_big, _pick(d, (512, 256, 128)), "out_proj")
        hmid, hpk, ids128, gates128 = _ln_router(
            z, x_all, ln1_g[l][None, :], ln1_b[l][None, :], w_router[l].astype(BF16),
            b_router[l][None, :].astype(F32), dn_alpha, _pick(t, (320, 256, 128, 64, 32, 16, 8)))

        n_rows = t * TOP_K + n_experts * ROW_SUB
        spi = ROWS_PER_ITEM // ROW_SUB
        n_items = (n_rows // ROW_SUB + (spi - 1) * n_experts) // spi + 1
        slot, tok_of_row, n_used, item_expert, item_row0, item_nsub = _route_layout(
            ids128[:, :TOP_K], n_experts, n_items, n_rows)
        ys = _moe_ffn(item_expert, item_row0, item_nsub, tok_of_row, n_used, hpk, perm,
                      w_gate_up, b_gate_up, w_down, b_down, l, n_rows)
        out_p, out_s = _combine_ln(slot.reshape(-1), ys, gates128, hmid,
                                   ln2_g[l][None, :], ln2_b[l][None, :], dn_alpha,
                                   _pick(math.gcd(tp, ts), (128, 64, 32, 16, 8)), tp)
        if l + 1 < depth:
            x_all = jnp.concatenate([out_p, out_s], axis=0)
        new_p.append((ckv[:tp].reshape(bp, sp, KV_LORA), kpe[:tp].reshape(bp, sp, QK_ROPE),
                      dk[:tp].reshape(bp, sp, 2, DIFF_HEAD), dv[:tp].reshape(bp, sp, DIFF_V)))
        new_s.append((ckv[tp:].reshape(bs, ss, KV_LORA), kpe[tp:].reshape(bs, ss, QK_ROPE),
                      dk[tp:].reshape(bs, ss, 2, DIFF_HEAD), dv[tp:].reshape(bs, ss, DIFF_V)))

    stack = lambda rows, i: jnp.stack([r_[i] for r_ in rows])
    return (out_p.reshape(bp, sp, d), out_s.reshape(bs, ss, d),
            stack(new_p, 0), stack(new_p, 1), stack(new_p, 2), stack(new_p, 3),
            stack(new_s, 0), stack(new_s, 1), stack(new_s, 2), stack(new_s, 3))
```

```python
import functools
import math

import jax
import jax.numpy as jnp
import numpy as np
from jax import lax
from jax.experimental import pallas as pl
from jax.experimental.pallas import tpu as pltpu

V_HEAD = 128
KV_LORA = 512
QK_NOPE = 128
QK_ROPE = 64
QK_ALL = KV_LORA + QK_ROPE
ROPE_THETA = 10000.0
MLA_SCALE = (QK_NOPE + QK_ROPE) ** -0.5
DIFF_HEAD = 128
DIFF_V = 2 * DIFF_HEAD
DIFF_SCALE = DIFF_HEAD ** -0.5
DKV_WIDTH = 2 * DIFF_V + 128
SUBLN_EPS = 1e-5
N_BUCKETS = 32
MAX_DISTANCE = 128
TOP_K = 4
SWIGLU_LIMIT = 7.0
SWIGLU_ALPHA = 1.702
LN_EPS = 1e-5
RMS_EPS = 1e-6

LANE = 128
VMEM_LIMIT = 60 * 1024 * 1024

ROW_SUB = 256
ROWS_PER_ITEM = 1280
FF_TILE = 256
DOWN_TILE = 512
PAGES_PER_STEP = 32

F32 = jnp.float32
BF16 = jnp.bfloat16


def _pick(n, cands):
    for c in cands:
        if n % c == 0:
            return c
    return n


def _cparams(sem, vmem=VMEM_LIMIT):
    return pltpu.CompilerParams(dimension_semantics=sem, vmem_limit_bytes=vmem)


def _mm_kernel(x_ref, w_ref, o_ref):
    o_ref[...] = jnp.dot(x_ref[...], w_ref[...], preferred_element_type=F32).astype(o_ref.dtype)


def _matmul(x, w, tm, tn, out_dtype, name):
    m, k = x.shape
    n = w.shape[1]
    return pl.pallas_call(
        _mm_kernel,
        grid=(m // tm, n // tn),
        in_specs=[pl.BlockSpec((tm, k), lambda i, j: (i, 0)),
                  pl.BlockSpec((k, tn), lambda i, j: (0, j))],
        out_specs=pl.BlockSpec((tm, tn), lambda i, j: (i, j)),
        out_shape=jax.ShapeDtypeStruct((m, n), out_dtype),
        compiler_params=_cparams(("parallel", "arbitrary")),
        name=name,
    )(x, w)


def _mm2_kernel(a_ref, b_ref, wa_ref, wb_ref, o_ref):
    acc = jnp.dot(a_ref[...], wa_ref[...], preferred_element_type=F32)
    acc += jnp.dot(b_ref[...], wb_ref[...], preferred_element_type=F32)
    o_ref[...] = acc


def _matmul2(a, b, wa, wb, tm, tn, name):
    m, ka = a.shape
    kb = b.shape[1]
    n = wa.shape[1]
    return pl.pallas_call(
        _mm2_kernel,
        grid=(m // tm, n // tn),
        in_specs=[pl.BlockSpec((tm, ka), lambda i, j: (i, 0)),
                  pl.BlockSpec((tm, kb), lambda i, j: (i, 0)),
                  pl.BlockSpec((ka, tn), lambda i, j: (0, j)),
                  pl.BlockSpec((kb, tn), lambda i, j: (0, j))],
        out_specs=pl.BlockSpec((tm, tn), lambda i, j: (i, j)),
        out_shape=jax.ShapeDtypeStruct((m, n), F32),
        compiler_params=_cparams(("parallel", "arbitrary")),
        name=name,
    )(a, b, wa, wb)


def _rope128(x, cos2, sin2):
    lane = lax.broadcasted_iota(jnp.int32, x.shape, 1)
    first_half = (lane & (QK_ROPE - 1)) < (QK_ROPE // 2)
    rot = jnp.where(first_half, pltpu.roll(x, LANE - QK_ROPE // 2, 1), pltpu.roll(x, QK_ROPE // 2, 1))
    return x * cos2 + rot * sin2


def _kv_post_kernel(hc_ref, hd_ref, hp_ref, g_ref, cos_ref, sin_ref,
                    ckv_ref, kpe_ref, dk_ref, dv_ref, kall_ref, dkv_ref):
    hc = hc_ref[...]
    ckv = hc * lax.rsqrt(jnp.mean(hc * hc, axis=-1, keepdims=True) + RMS_EPS) * g_ref[...]
    ckv_ref[...] = ckv
    kpe = _rope128(hp_ref[...], cos_ref[...], sin_ref[...])[:, :QK_ROPE]
    kpe_ref[...] = kpe
    kall_ref[:, :KV_LORA] = ckv.astype(BF16)
    kall_ref[:, KV_LORA:] = kpe.astype(BF16)
    hd = hd_ref[...]
    dk_ref[...] = hd[:, :DIFF_V]
    dv_ref[...] = hd[:, DIFF_V:]
    dkv_ref[:, :2 * DIFF_V] = hd.astype(BF16)
    dkv_ref[:, 2 * DIFF_V:] = jnp.ones((hd.shape[0], DKV_WIDTH - 2 * DIFF_V), BF16)


def _kv_post(h2, g_kv, cos2, sin2, q_lora, pe_off, tm):
    t = h2.shape[0]
    row = lambda i: (i, 0)
    return pl.pallas_call(
        _kv_post_kernel,
        grid=(t // tm,),
        in_specs=[pl.BlockSpec((tm, KV_LORA), lambda i: (i, q_lora // KV_LORA)),
                  pl.BlockSpec((tm, 2 * DIFF_V), lambda i: (i, (q_lora + KV_LORA) // (2 * DIFF_V))),
                  pl.BlockSpec((tm, LANE), lambda i: (i, pe_off // LANE)),
                  pl.BlockSpec((1, KV_LORA), lambda i: (0, 0)),
                  pl.BlockSpec((tm, LANE), row),
                  pl.BlockSpec((tm, LANE), row)],
        out_specs=[pl.BlockSpec((tm, KV_LORA), row),
                   pl.BlockSpec((tm, QK_ROPE), row),
                   pl.BlockSpec((tm, DIFF_V), row),
                   pl.BlockSpec((tm, DIFF_V), row),
                   pl.BlockSpec((tm, QK_ALL), row),
                   pl.BlockSpec((tm, DKV_WIDTH), row)],
        out_shape=[jax.ShapeDtypeStruct((t, KV_LORA), F32),
                   jax.ShapeDtypeStruct((t, QK_ROPE), F32),
                   jax.ShapeDtypeStruct((t, DIFF_V), F32),
                   jax.ShapeDtypeStruct((t, DIFF_V), F32),
                   jax.ShapeDtypeStruct((t, QK_ALL), BF16),
                   jax.ShapeDtypeStruct((t, DKV_WIDTH), BF16)],
        compiler_params=_cparams(("parallel",)),
        name="kv_post",
    )(h2, h2, h2, g_kv, cos2, sin2)


def _q_proj_kernel(hq_ref, g_ref, wuq_ref, wk_ref, cos_ref, sin_ref, o_ref, *, n_heads):
    hq = hq_ref[...]
    qn = hq * lax.rsqrt(jnp.mean(hq * hq, axis=-1, keepdims=True) + RMS_EPS) * g_ref[...]
    q = jnp.dot(qn.astype(BF16), wuq_ref[...], preferred_element_type=F32)
    for h in range(n_heads):
        qh = q[:, h * QK_NOPE:(h + 1) * QK_NOPE].astype(BF16)
        o_ref[h, :, :KV_LORA] = jnp.dot(qh, wk_ref[h], preferred_element_type=F32).astype(BF16)
    pe0 = n_heads * QK_NOPE
    cos2, sin2 = cos_ref[...], sin_ref[...]
    for g in range(n_heads // 2):
        r = _rope128(q[:, pe0 + g * LANE: pe0 + (g + 1) * LANE], cos2, sin2).astype(BF16)
        o_ref[2 * g, :, KV_LORA:] = r[:, :QK_ROPE]
        o_ref[2 * g + 1, :, KV_LORA:] = r[:, QK_ROPE:]


def _q_proj(h2, g_q, wuq, wk, cos2, sin2, q_lora, n_heads, tm):
    t = h2.shape[0]
    return pl.pallas_call(
        functools.partial(_q_proj_kernel, n_heads=n_heads),
        grid=(t // tm,),
        in_specs=[pl.BlockSpec((tm, q_lora), lambda i: (i, 0)),
                  pl.BlockSpec((1, q_lora), lambda i: (0, 0)),
                  pl.BlockSpec(wuq.shape, lambda i: (0, 0)),
                  pl.BlockSpec(wk.shape, lambda i: (0, 0, 0)),
                  pl.BlockSpec((tm, LANE), lambda i: (i, 0)),
                  pl.BlockSpec((tm, LANE), lambda i: (i, 0))],
        out_specs=pl.BlockSpec((n_heads, tm, QK_ALL), lambda i: (0, i, 0)),
        out_shape=jax.ShapeDtypeStruct((n_heads, t, QK_ALL), BF16),
        compiler_params=_cparams(("parallel",)),
        name="q_proj",
    )(h2, g_q, wuq, wk, cos2, sin2)


def _sm_step(s, v_bf, m_ref, l_ref, acc_ref):
    m_old = m_ref[...]
    m_new = jnp.maximum(m_old, jnp.max(s, axis=-1, keepdims=True))
    corr = jnp.exp(m_old - m_new)
    p = jnp.exp(s - m_new)
    l_ref[...] = l_ref[...] * corr + jnp.sum(p, axis=-1, keepdims=True)
    acc_ref[...] = acc_ref[...] * corr + jnp.dot(p.astype(BF16), v_bf, preferred_element_type=F32)
    m_ref[...] = m_new


def _sm_reset(m_ref, l_ref, acc_ref):
    m_ref[...] = jnp.full(m_ref.shape, -jnp.inf, F32)
    l_ref[...] = jnp.zeros(l_ref.shape, F32)
    acc_ref[...] = jnp.zeros(acc_ref.shape, F32)


_NT = (((1,), (1,)), ((), ()))


def _mla_prompt_kernel(q_ref, k_ref, wv_ref, o_ref, m_ref, l_ref, acc_ref, *, blk, n_grp):
    i = pl.program_id(1)
    q = q_ref[...].reshape(n_grp * blk, QK_ALL)
    _sm_reset(m_ref, l_ref, acc_ref)

    def chunk(j):
        return pl.ds(pl.multiple_of(j * blk, blk), blk)

    def logits(j):
        return lax.dot_general(q, k_ref[chunk(j), :], _NT, preferred_element_type=F32) * MLA_SCALE

    def body(j, s):
        s_next = logits(j + 1)
        _sm_step(s, k_ref[chunk(j), pl.ds(0, KV_LORA)], m_ref, l_ref, acc_ref)
        return s_next

    s = lax.fori_loop(0, i, body, logits(0))
    row = lax.broadcasted_iota(jnp.int32, s.shape, 0) & (blk - 1)
    col = lax.broadcasted_iota(jnp.int32, s.shape, 1)
    _sm_step(jnp.where(row >= col, s, -jnp.inf), k_ref[chunk(i), pl.ds(0, KV_LORA)], m_ref, l_ref, acc_ref)
    o_lat = (acc_ref[...] / l_ref[...]).astype(BF16)
    for g in range(n_grp):
        o_ref[:, g * V_HEAD:(g + 1) * V_HEAD] = jnp.dot(
            o_lat[g * blk:(g + 1) * blk], wv_ref[g], preferred_element_type=F32).astype(o_ref.dtype)


def _mla_prompt(q_all, kall, wv, batch, seq, blk, n_grp):
    n_heads = q_all.shape[0]
    nq = seq // blk
    rows = n_grp * blk
    return pl.pallas_call(
        functools.partial(_mla_prompt_kernel, blk=blk, n_grp=n_grp),
        grid=(batch, nq, n_heads // n_grp),
        in_specs=[pl.BlockSpec((n_grp, blk, QK_ALL), lambda b, i, h: (h, b * nq + i, 0)),
                  pl.BlockSpec((seq, QK_ALL), lambda b, i, h: (b, 0)),
                  pl.BlockSpec((n_grp, KV_LORA, V_HEAD), lambda b, i, h: (h, 0, 0))],
        out_specs=pl.BlockSpec((blk, n_grp * V_HEAD), lambda b, i, h: (b * nq + i, h)),
        out_shape=jax.ShapeDtypeStruct((batch * seq, n_heads * V_HEAD), BF16),
        scratch_shapes=[pltpu.VMEM((rows, 1), F32), pltpu.VMEM((rows, 1), F32),
                        pltpu.VMEM((rows, KV_LORA), F32)],
        compiler_params=_cparams(("parallel", "parallel", "arbitrary")),
        name="mla_prompt",
    )(q_all, kall, wv)


def _diff_lambda(dl, lam_init):
    s1 = jnp.sum(dl[0:1, :] * dl[1:2, :], axis=-1, keepdims=True)
    s2 = jnp.sum(dl[2:3, :] * dl[3:4, :], axis=-1, keepdims=True)
    return jnp.exp(s1) - jnp.exp(s2) + lam_init


def _diff_merge(o1, o2, lam, g_sub, lam_init):
    d = o1 - lam * o2
    d = d * lax.rsqrt(jnp.mean(d * d, axis=-1, keepdims=True) + SUBLN_EPS) * g_sub
    return d * (1.0 - lam_init)


def _diff_prompt_kernel(q1_ref, q2_ref, kv_ref, bias_ref, dl_ref, g_ref, o_ref,
                        m1, a1, m2, a2, *, blk, n_grp, lam_init):
    i = pl.program_id(1)

    def stack(ref):
        x = ref[...].astype(BF16)
        return jnp.concatenate([x[:, g * DIFF_HEAD:(g + 1) * DIFF_HEAD] for g in range(n_grp)], axis=0)

    q1 = stack(q1_ref)
    q2 = stack(q2_ref)
    for m_ref, a_ref in ((m1, a1), (m2, a2)):
        m_ref[...] = jnp.full(m_ref.shape, -jnp.inf, F32)
        a_ref[...] = jnp.zeros(a_ref.shape, F32)

    def step(s, v_aug, m_ref, a_ref):
        m_old = m_ref[...]
        m_new = jnp.maximum(m_old, jnp.max(s, axis=-1, keepdims=True))
        p = jnp.exp(s - m_new)
        a_ref[...] = a_ref[...] * jnp.exp(m_old - m_new) + jnp.dot(
            p.astype(BF16), v_aug, preferred_element_type=F32)
        m_ref[...] = m_new

    def chunk(j):
        return pl.ds(pl.multiple_of(j * blk, blk), blk)

    def logits(j):
        s1 = lax.dot_general(q1, kv_ref[chunk(j), pl.ds(0, DIFF_HEAD)], _NT, preferred_element_type=F32)
        s2 = lax.dot_general(q2, kv_ref[chunk(j), pl.ds(DIFF_HEAD, DIFF_HEAD)], _NT,
                             preferred_element_type=F32)
        return s1 * DIFF_SCALE, s2 * DIFF_SCALE

    def values(j):
        return kv_ref[chunk(j), pl.ds(DIFF_V, DKV_WIDTH - DIFF_V)]

    def body(j, s):
        s_next = logits(j + 1)
        b = bias_ref[jnp.where(j == i - 1, 1, 2)].reshape(n_grp * blk, blk)
        v = values(j)
        step(s[0] + b, v, m1, a1)
        step(s[1] + b, v, m2, a2)
        return s_next

    s1, s2 = lax.fori_loop(0, i, body, logits(0))
    row = lax.broadcasted_iota(jnp.int32, s1.shape, 0) & (blk - 1)
    col = lax.broadcasted_iota(jnp.int32, s1.shape, 1)
    keep = row >= col
    b = bias_ref[0].reshape(n_grp * blk, blk)
    v = values(i)
    step(jnp.where(keep, s1 + b, -jnp.inf), v, m1, a1)
    step(jnp.where(keep, s2 + b, -jnp.inf), v, m2, a2)

    def normalised(a_ref):
        a = a_ref[...]
        den = a[:, DIFF_V:]
        return a[:, :DIFF_V] / jnp.concatenate([den] * (DIFF_V // LANE), axis=1)

    lam = _diff_lambda(dl_ref[...], lam_init)
    o = _diff_merge(normalised(a1), normalised(a2), lam, g_ref[...], lam_init).astype(o_ref.dtype)
    for g in range(n_grp):
        o_ref[:, g * DIFF_V:(g + 1) * DIFF_V] = o[g * blk:(g + 1) * blk]


def _diff_prompt(h2, dkv, bias_tiles, dl, g_sub, batch, seq, blk, qd_off, n_dh, n_grp, lam_init):
    nq = seq // blk
    width = n_grp * DIFF_HEAD
    assert qd_off % width == 0 and (n_dh * DIFF_HEAD) % width == 0
    c1 = qd_off // width
    c2 = (qd_off + n_dh * DIFF_HEAD) // width
    rows = n_grp * blk
    return pl.pallas_call(
        functools.partial(_diff_prompt_kernel, blk=blk, n_grp=n_grp, lam_init=lam_init),
        grid=(batch, nq, n_dh // n_grp),
        in_specs=[pl.BlockSpec((blk, width), lambda b, i, h: (b * nq + i, c1 + h)),
                  pl.BlockSpec((blk, width), lambda b, i, h: (b * nq + i, c2 + h)),
                  pl.BlockSpec((seq, DKV_WIDTH), lambda b, i, h: (b, 0)),
                  pl.BlockSpec((3, n_grp, blk, blk), lambda b, i, h: (0, h, 0, 0)),
                  pl.BlockSpec((4, DIFF_HEAD), lambda b, i, h: (0, 0)),
                  pl.BlockSpec((1, DIFF_V), lambda b, i, h: (0, 0))],
        out_specs=pl.BlockSpec((blk, n_grp * DIFF_V), lambda b, i, h: (b * nq + i, h)),
        out_shape=jax.ShapeDtypeStruct((batch * seq, n_dh * DIFF_V), BF16),
        scratch_shapes=[pltpu.VMEM((rows, 1), F32), pltpu.VMEM((rows, DIFF_V + LANE), F32),
                        pltpu.VMEM((rows, 1), F32), pltpu.VMEM((rows, DIFF_V + LANE), F32)],
        compiler_params=_cparams(("parallel", "parallel", "arbitrary")),
        name="diff_prompt",
    )(h2, h2, dkv, bias_tiles, dl, g_sub)


def _decode_kernel(pt_ref, qm_ref, qd_ref, ks_ref, dks_ref, bfar_ref, blast_ref, bself_ref,
                   ckv_hbm, kpe_hbm, dk_hbm, dv_hbm,
                   olat_ref, od_ref,
                   ckv_buf, kpe_buf, dk_buf, dv_buf, sems,
                   mm, lm, am, md, ld, ad, *, layer, n_groups, n_batch, gp, page, n_dh):
    b = pl.program_id(0)
    g = pl.program_id(1)
    step = b * n_groups + g
    slot = step % 2

    def copies(bb, gg, sl):
        out = []
        for j in range(gp):
            phys = pt_ref[bb, gg * gp + j]
            out.append(pltpu.make_async_copy(ckv_hbm.at[layer, phys], ckv_buf.at[sl, j], sems.at[sl, 0]))
            out.append(pltpu.make_async_copy(kpe_hbm.at[layer, phys],
                                             kpe_buf.at[sl, :, pl.ds(j * page, page)], sems.at[sl, 1]))
            out.append(pltpu.make_async_copy(dk_hbm.at[layer, phys, :, 0, :],
                                             dk_buf.at[sl, j, :, pl.ds(0, DIFF_HEAD)], sems.at[sl, 2]))
            out.append(pltpu.make_async_copy(dk_hbm.at[layer, phys, :, 1, :],
                                             dk_buf.at[sl, j, :, pl.ds(DIFF_HEAD, DIFF_HEAD)], sems.at[sl, 2]))
            out.append(pltpu.make_async_copy(dv_hbm.at[layer, phys], dv_buf.at[sl, j], sems.at[sl, 3]))
        return out

    @pl.when(step == 0)
    def _():
        for c in copies(0, 0, 0):
            c.start()

    nxt = step + 1

    @pl.when(nxt < n_batch * n_groups)
    def _():
        for c in copies(nxt // n_groups, nxt % n_groups, nxt % 2):
            c.start()

    @pl.when(g == 0)
    def _():
        _sm_reset(mm, lm, am)
        _sm_reset(md, ld, ad)

    for c in copies(b, g, slot):
        c.wait()

    nk = gp * page
    qm = qm_ref[...]
    qd = qd_ref[...]
    kc = ckv_buf[slot].reshape(nk, KV_LORA).astype(BF16)
    kp_t = kpe_buf[slot].astype(BF16)
    s = lax.dot_general(qm[:, :KV_LORA], kc, _NT, preferred_element_type=F32)
    s += jnp.dot(qm[:, KV_LORA:], kp_t, preferred_element_type=F32)
    _sm_step(s * MLA_SCALE, kc, mm, lm, am)

    dk = dk_buf[slot].reshape(nk, DIFF_V).astype(BF16)
    dv = dv_buf[slot].reshape(nk, DIFF_V).astype(BF16)
    s1 = lax.dot_general(qd[:n_dh], dk[:, :DIFF_HEAD], _NT, preferred_element_type=F32)
    s2 = lax.dot_general(qd[n_dh:], dk[:, DIFF_HEAD:], _NT, preferred_element_type=F32)
    bias = jnp.where(g == n_groups - 1, blast_ref[...], bfar_ref[...])
    sd = jnp.concatenate([s1, s2], axis=0) * DIFF_SCALE + bias
    _sm_step(sd, dv, md, ld, ad)

    @pl.when(g == n_groups - 1)
    def _():
        ks = ks_ref[...].astype(F32)
        s_self = jnp.sum(qm.astype(F32) * ks, axis=-1, keepdims=True) * MLA_SCALE
        m_old = mm[...]
        m_new = jnp.maximum(m_old, s_self)
        corr = jnp.exp(m_old - m_new)
        p = jnp.exp(s_self - m_new)
        l_new = lm[...] * corr + p
        olat_ref[...] = (am[...] * corr + p * ks[:, :KV_LORA]) / l_new

        dks = dks_ref[...].astype(F32)
        qdf = qd.astype(F32)
        t1 = jnp.sum(qdf[:n_dh] * dks[:, :DIFF_HEAD], axis=-1, keepdims=True)
        t2 = jnp.sum(qdf[n_dh:] * dks[:, DIFF_HEAD:DIFF_V], axis=-1, keepdims=True)
        sd_self = jnp.concatenate([t1, t2], axis=0) * DIFF_SCALE + bself_ref[:, 0:1]
        m_old = md[...]
        m_new = jnp.maximum(m_old, sd_self)
        corr = jnp.exp(m_old - m_new)
        p = jnp.exp(sd_self - m_new)
        l_new = ld[...] * corr + p
        od_ref[...] = (ad[...] * corr + p * dks[:, DIFF_V:]) / l_new


def _decode(page_table, qm, qd, kall_s, dkv_s, bfar, blast, bself,
            cache_ckv, cache_kpe, cache_dk, cache_dv, layer, gp):
    n_batch, n_heads, _ = qm.shape
    n_maps = qd.shape[1]
    n_pages = page_table.shape[1]
    page = cache_ckv.shape[2]
    n_groups = n_pages // gp
    nk = gp * page
    per_b = lambda b, g, pt: (b, 0, 0)
    const2 = lambda b, g, pt: (0, 0)
    any_spec = pl.BlockSpec(memory_space=pl.ANY)
    grid_spec = pltpu.PrefetchScalarGridSpec(
        num_scalar_prefetch=1,
        grid=(n_batch, n_groups),
        in_specs=[pl.BlockSpec((None, n_heads, QK_ALL), per_b),
                  pl.BlockSpec((None, n_maps, DIFF_HEAD), per_b),
                  pl.BlockSpec((None, 1, QK_ALL), per_b),
                  pl.BlockSpec((None, 1, 2 * DIFF_V), per_b),
                  pl.BlockSpec((n_maps, nk), const2),
                  pl.BlockSpec((n_maps, nk), const2),
                  pl.BlockSpec((n_maps, LANE), const2),
                  any_spec, any_spec, any_spec, any_spec],
        out_specs=[pl.BlockSpec((None, n_heads, KV_LORA), per_b),
                   pl.BlockSpec((None, n_maps, DIFF_V), per_b)],
        scratch_shapes=[pltpu.VMEM((2, gp, page, KV_LORA), F32),
                        pltpu.VMEM((2, QK_ROPE, gp * page), F32),
                        pltpu.VMEM((2, gp, page, DIFF_V), F32),
                        pltpu.VMEM((2, gp, page, DIFF_V), F32),
                        pltpu.SemaphoreType.DMA((2, 4)),
                        pltpu.VMEM((n_heads, 1), F32), pltpu.VMEM((n_heads, 1), F32),
                        pltpu.VMEM((n_heads, KV_LORA), F32),
                        pltpu.VMEM((n_maps, 1), F32), pltpu.VMEM((n_maps, 1), F32),
                        pltpu.VMEM((n_maps, DIFF_V), F32)])
    return pl.pallas_call(
        functools.partial(_decode_kernel, layer=layer, n_groups=n_groups, n_batch=n_batch, gp=gp,
                          page=page, n_dh=n_maps // 2),
        grid_spec=grid_spec,
        out_shape=[jax.ShapeDtypeStruct((n_batch, n_heads, KV_LORA), F32),
                   jax.ShapeDtypeStruct((n_batch, n_maps, DIFF_V), F32)],
        compiler_params=_cparams(("arbitrary", "arbitrary")),
        name="decode_attn",
    )(page_table, qm, qd, kall_s, dkv_s, bfar, blast, bself, cache_ckv, cache_kpe, cache_dk, cache_dv)


def _decode_merge_kernel(olat_ref, wv_ref, od_ref, dl_ref, g_ref, omla_ref, odiff_ref, *, n_dh, lam_init):
    h = pl.program_id(0)
    omla_ref[...] = jnp.dot(olat_ref[...].astype(BF16), wv_ref[...],
                            preferred_element_type=F32).astype(omla_ref.dtype)

    @pl.when(h < n_dh)
    def _():
        lam = _diff_lambda(dl_ref[...], lam_init)
        odiff_ref[...] = _diff_merge(od_ref[0], od_ref[1], lam, g_ref[...], lam_init).astype(odiff_ref.dtype)


def _decode_merge(olat_h, wv, od_h, dl, g_sub, lam_init):
    n_heads, n_batch, _ = olat_h.shape
    n_dh = od_h.shape[1]
    return pl.pallas_call(
        functools.partial(_decode_merge_kernel, n_dh=n_dh, lam_init=lam_init),
        grid=(n_heads,),
        in_specs=[pl.BlockSpec((None, n_batch, KV_LORA), lambda h: (h, 0, 0)),
                  pl.BlockSpec((None, KV_LORA, V_HEAD), lambda h: (h, 0, 0)),
                  pl.BlockSpec((2, None, n_batch, DIFF_V), lambda h: (0, jnp.minimum(h, n_dh - 1), 0, 0)),
                  pl.BlockSpec((4, DIFF_HEAD), lambda h: (0, 0)),
                  pl.BlockSpec((1, DIFF_V), lambda h: (0, 0))],
        out_specs=[pl.BlockSpec((n_batch, V_HEAD), lambda h: (0, h)),
                   pl.BlockSpec((n_batch, DIFF_V), lambda h: (0, jnp.minimum(h, n_dh - 1)))],
        out_shape=[jax.ShapeDtypeStruct((n_batch, n_heads * V_HEAD), BF16),
                   jax.ShapeDtypeStruct((n_batch, n_dh * DIFF_V), BF16)],
        compiler_params=_cparams(("arbitrary",)),
        name="decode_merge",
    )(olat_h, wv, od_h, dl, g_sub)


def _layernorm(z, g, b):
    mu = jnp.mean(z, axis=-1, keepdims=True)
    zc = z - mu
    var = jnp.mean(zc * zc, axis=-1, keepdims=True)
    return zc * lax.rsqrt(var + LN_EPS) * g + b


def _ln_router_kernel(z_ref, x_ref, g_ref, b_ref, wr_ref, br_ref, h_ref, hpk_ref, ids_ref, gates_ref,
                      *, alpha, n_experts):
    hmid = _layernorm(alpha * x_ref[...] + z_ref[...], g_ref[...], b_ref[...])
    h_ref[...] = hmid
    hbf = hmid.astype(BF16)
    bits = lax.bitcast_convert_type(hbf.astype(F32), jnp.uint32)
    half = bits.shape[1] // 2
    hpk_ref[...] = (bits[:, :half] >> 16) | (bits[:, half:] & jnp.uint32(0xFFFF0000))
    logits = jnp.dot(hbf, wr_ref[...], preferred_element_type=F32) + br_ref[...]
    lane = lax.broadcasted_iota(jnp.int32, logits.shape, 1)
    out_lane = lax.broadcasted_iota(jnp.int32, ids_ref.shape, 1)
    ids = jnp.zeros(ids_ref.shape, jnp.int32)
    vals = jnp.zeros(gates_ref.shape, F32)
    top = None
    denom = None
    for k in range(TOP_K):
        v = jnp.max(logits, axis=-1, keepdims=True)
        idx = jnp.min(jnp.where(logits == v, lane, n_experts), axis=-1, keepdims=True)
        logits = jnp.where(lane == idx, -jnp.inf, logits)
        if k == 0:
            top = v
        e = jnp.exp(v - top)
        denom = e if k == 0 else denom + e
        ids = jnp.where(out_lane == k, idx, ids)
        vals = jnp.where(out_lane == k, e, vals)
    ids_ref[...] = ids
    gates_ref[...] = vals / denom


def _ln_router(z, x, g, b, wr, br, alpha, tm):
    t, d = z.shape
    n_experts = wr.shape[1]
    row = lambda i: (i, 0)
    c = lambda i: (0, 0)
    return pl.pallas_call(
        functools.partial(_ln_router_kernel, alpha=alpha, n_experts=n_experts),
        grid=(t // tm,),
        in_specs=[pl.BlockSpec((tm, d), row), pl.BlockSpec((tm, d), row),
                  pl.BlockSpec((1, d), c), pl.BlockSpec((1, d), c),
                  pl.BlockSpec((d, n_experts), c), pl.BlockSpec((1, n_experts), c)],
        out_specs=[pl.BlockSpec((tm, d), row), pl.BlockSpec((tm, d // 2), row),
                   pl.BlockSpec((tm, LANE), row), pl.BlockSpec((tm, LANE), row)],
        out_shape=[jax.ShapeDtypeStruct((t, d), F32), jax.ShapeDtypeStruct((t, d // 2), jnp.uint32),
                   jax.ShapeDtypeStruct((t, LANE), jnp.int32), jax.ShapeDtypeStruct((t, LANE), F32)],
        compiler_params=_cparams(("parallel",)),
        name="ln_router",
    )(z, x, g, b, wr, br)


def _moe_kernel(ex_ref, r0_ref, ns_ref, tok_ref, used_ref,
                hpk_hbm, perm_ref, wgu_ref, bgu_ref, wd_ref, bd_ref, ys_hbm,
                xraw, act, ybuf, xsem, ysem, *, n_ff, n_down, n_items, chunk):
    w = pl.program_id(0)
    s = pl.program_id(1)
    nsub = ns_ref[w]
    row0 = r0_ref[w]
    half = xraw.shape[1]
    nxt = jnp.minimum(w + 1, n_items - 1)
    chunks_per_sub = ROW_SUB // chunk

    def issue_chunk(base, k):
        for u in range(chunk):
            r = k * chunk + u
            pltpu.make_async_copy(hpk_hbm.at[pl.ds(tok_ref[base + r], 1)], xraw.at[pl.ds(r, 1)],
                                  xsem.at[0]).start()

    def issue_chunks(base, lo, hi):
        def go(k, c):
            issue_chunk(base, k)
            return c
        lax.fori_loop(lo, hi, go, 0)

    def wait_chunks(n):
        def go(k, c):
            pltpu.make_async_copy(hpk_hbm.at[pl.ds(0, chunk)], xraw.at[pl.ds(0, chunk)], xsem.at[0]).wait()
            return c
        lax.fori_loop(0, n, go, 0)

    @pl.when(s == 0)
    def _():
        @pl.when(w == 0)
        def _():
            issue_chunks(row0, 0, nsub * chunks_per_sub)
            wait_chunks(nsub * chunks_per_sub)

        @pl.when(w > 0)
        def _():
            wait_chunks(jnp.maximum(ns_ref[jnp.maximum(w - 1, 0)], nsub) * chunks_per_sub)

    @pl.when(s < n_ff)
    def _():
        bias = bgu_ref[...]
        perm = perm_ref[...]
        lane = lax.broadcasted_iota(jnp.int32, (ROW_SUB, LANE), 1)
        even = (lane & 1) == 0

        def weights():
            return (wgu_ref[pl.ds(0, half), :].astype(BF16), wgu_ref[pl.ds(half, half), :].astype(BF16))

        def sub(r, w_lo, w_hi):
            rows = pl.ds(pl.multiple_of(r * ROW_SUB, ROW_SUB), ROW_SUB)
            xr = xraw[rows, :]
            x_lo = lax.bitcast_convert_type(xr << 16, F32).astype(BF16)
            x_hi = lax.bitcast_convert_type(xr & jnp.uint32(0xFFFF0000), F32).astype(BF16)
            hgu = jnp.dot(x_lo, w_lo, preferred_element_type=F32)
            hgu = hgu + jnp.dot(x_hi, w_hi, preferred_element_type=F32) + bias
            for q in range(FF_TILE // LANE):
                a = hgu[:, (2 * q) * LANE:(2 * q + 1) * LANE]
                bq = hgu[:, (2 * q + 1) * LANE:(2 * q + 2) * LANE]
                gate = jnp.where(even, a, pltpu.roll(bq, 1, 1))
                up = jnp.where(even, pltpu.roll(a, LANE - 1, 1), bq)
                gate = jnp.minimum(gate, SWIGLU_LIMIT)
                up = jnp.clip(up, -SWIGLU_LIMIT, SWIGLU_LIMIT)
                mixed = ((up + 1.0) * gate * jax.nn.sigmoid(SWIGLU_ALPHA * gate)).astype(BF16)
                act[s, rows, q * LANE:(q + 1) * LANE] = jnp.dot(
                    mixed, perm, preferred_element_type=F32).astype(BF16)

        def pair(i, c):
            w_lo, w_hi = weights()
            sub(2 * i, w_lo, w_hi)
            sub(2 * i + 1, w_lo, w_hi)
            return c
        lax.fori_loop(0, nsub // 2, pair, 0)

        @pl.when(nsub % 2 == 1)
        def _():
            sub(nsub - 1, *weights())

    def y_copy(first_row, n, sl):
        dst = ys_hbm.at[pl.ds(pl.multiple_of(first_row, ROW_SUB), ROW_SUB),
                        pl.ds(pl.multiple_of(n * DOWN_TILE, DOWN_TILE), DOWN_TILE)]
        return pltpu.make_async_copy(ybuf.at[sl], dst, ysem.at[sl])

    @pl.when(s >= n_ff)
    def _():
        n = s - n_ff
        bias = bd_ref[...]
        nxt_row0 = r0_ref[nxt]

        def weights():
            return [wd_ref[pl.ds(j * FF_TILE, FF_TILE), :].astype(BF16) for j in range(n_ff)]

        def sub(r, sl, wd_bf):
            rows = pl.ds(pl.multiple_of(r * ROW_SUB, ROW_SUB), ROW_SUB)
            y = bias
            for j in range(n_ff):
                y = y + jnp.dot(act[j, rows, :], wd_bf[j], preferred_element_type=F32)
            ybuf[sl] = y
            y_copy(row0 + r * ROW_SUB, n, sl).start()

        def pair(i, reuse):
            issue_chunk(nxt_row0, n * nsub + 2 * i)
            issue_chunk(nxt_row0, n * nsub + 2 * i + 1)
            if reuse:
                y_copy(row0, n, 0).wait()
                y_copy(row0, n, 1).wait()
            wd_bf = weights()
            sub(2 * i, 0, wd_bf)
            sub(2 * i + 1, 1, wd_bf)

        @pl.when(nsub >= 2)
        def _():
            pair(0, False)

        def later_pair(i, c):
            pair(i, True)
            return c
        lax.fori_loop(1, nsub // 2, later_pair, 0)

        @pl.when(nsub % 2 == 1)
        def _():
            issue_chunk(nxt_row0, n * nsub + nsub - 1)

            @pl.when(nsub >= 3)
            def _():
                y_copy(row0, n, 0).wait()
            sub(nsub - 1, 0, weights())

        @pl.when(nsub >= 2)
        def _():
            y_copy(row0, n, nsub % 2).wait()

        @pl.when(nsub >= 1)
        def _():
            y_copy(row0, n, (nsub - 1) % 2).wait()

        @pl.when(s == n_ff + n_down - 1)
        def _():
            issue_chunks(nxt_row0, nsub * n_down, ns_ref[nxt] * n_down)

    @pl.when((w == n_items - 1) & (s == n_ff + n_down - 1))
    def _():
        ybuf[0] = jnp.zeros((ROW_SUB, DOWN_TILE), F32)
        first = used_ref[0]
        total = ys_hbm.shape[0] // ROW_SUB

        def fill(tile, c):
            for n in range(n_down):
                y_copy(tile * ROW_SUB, n, 0).start()
            return c
        lax.fori_loop(first, total, fill, 0)

        def drain(tile, c):
            for n in range(n_down):
                y_copy(tile * ROW_SUB, n, 0).wait()
            return c
        lax.fori_loop(first, total, drain, 0)


def _moe_ffn(item_expert, item_row0, item_nsub, tok_of_row, n_used, hpk, perm,
             w_gate_up, b_gate_up, w_down, b_down, layer, n_rows):
    d_model = 2 * hpk.shape[1]
    n_experts, _, two_de = w_gate_up.shape[1:]
    d_expert = two_de // 2
    n_ff = d_expert // FF_TILE
    n_down = d_model // DOWN_TILE
    n_items = item_expert.shape[0]
    chunk = ROW_SUB // n_down
    bgu = b_gate_up.reshape(b_gate_up.shape[0], n_experts, 1, two_de)
    bd = b_down.reshape(b_down.shape[0], n_experts, 1, d_model)

    def wgu_map(w, s, ex, r0, ns, tok, used):
        return (layer, ex[w], 0, jnp.where(ns[w] > 0, jnp.minimum(s, n_ff - 1), n_ff - 1))

    def wd_map(w, s, ex, r0, ns, tok, used):
        return (layer, ex[w], 0, jnp.where(ns[w] > 0, jnp.maximum(s - n_ff, 0), n_down - 1))

    grid_spec = pltpu.PrefetchScalarGridSpec(
        num_scalar_prefetch=5,
        grid=(n_items, n_ff + n_down),
        in_specs=[pl.BlockSpec(memory_space=pl.ANY),
                  pl.BlockSpec((LANE, LANE), lambda w, s, *_: (0, 0)),
                  pl.BlockSpec((None, None, d_model, 2 * FF_TILE), wgu_map),
                  pl.BlockSpec((None, None, 1, 2 * FF_TILE), wgu_map),
                  pl.BlockSpec((None, None, d_expert, DOWN_TILE), wd_map),
                  pl.BlockSpec((None, None, 1, DOWN_TILE), wd_map)],
        out_specs=pl.BlockSpec(memory_space=pl.ANY),
        scratch_shapes=[pltpu.VMEM((ROWS_PER_ITEM, d_model // 2), jnp.uint32),
                        pltpu.VMEM((n_ff, ROWS_PER_ITEM, FF_TILE), BF16),
                        pltpu.VMEM((2, ROW_SUB, DOWN_TILE), F32),
                        pltpu.SemaphoreType.DMA((1,)),
                        pltpu.SemaphoreType.DMA((2,))])
    return pl.pallas_call(
        functools.partial(_moe_kernel, n_ff=n_ff, n_down=n_down, n_items=n_items, chunk=chunk),
        grid_spec=grid_spec,
        out_shape=jax.ShapeDtypeStruct((n_rows, d_model), F32),
        compiler_params=_cparams(("arbitrary", "arbitrary")),
        name="moe_ffn",
    )(item_expert, item_row0, item_nsub, tok_of_row, n_used, hpk, perm, w_gate_up, bgu, w_down, bd)


def _combine_ln_kernel(slot_ref, ys_hbm, gates_ref, h_ref, g_ref, b_ref, op_ref, os_ref, ybuf, sem,
                       *, alpha, n_prompt_tiles, n_tiles, tm):
    i = pl.program_id(0)
    cur = i % 2

    def issue(tile, buf):
        def go(r, c):
            for k in range(TOP_K):
                src = ys_hbm.at[pl.ds(slot_ref[(tile * tm + r) * TOP_K + k], 1)]
                pltpu.make_async_copy(src, ybuf.at[buf, k, pl.ds(r, 1)], sem.at[buf]).start()
            return c
        lax.fori_loop(0, tm, go, 0)

    @pl.when(i == 0)
    def _():
        issue(0, 0)

    @pl.when(i + 1 < n_tiles)
    def _():
        issue(i + 1, (i + 1) % 2)

    for k in range(TOP_K):
        pltpu.make_async_copy(ys_hbm.at[pl.ds(0, tm)], ybuf.at[cur, k], sem.at[cur]).wait()

    gates = gates_ref[...]
    ffn = gates[:, 0:1] * ybuf[cur, 0]
    for k in range(1, TOP_K):
        ffn = ffn + gates[:, k:k + 1] * ybuf[cur, k]
    out = _layernorm(alpha * h_ref[...] + ffn, g_ref[...], b_ref[...])

    @pl.when(i < n_prompt_tiles)
    def _():
        op_ref[...] = out

    @pl.when(i >= n_prompt_tiles)
    def _():
        os_ref[...] = out


def _combine_ln(slot_flat, ys, gates, hmid, g, b, alpha, tm, tp):
    t, d = hmid.shape
    assert tp % tm == 0 and (t - tp) % tm == 0
    npt = tp // tm
    n_tiles = t // tm
    row = lambda i, sl: (i, 0)
    c = lambda i, sl: (0, 0)
    grid_spec = pltpu.PrefetchScalarGridSpec(
        num_scalar_prefetch=1,
        grid=(n_tiles,),
        in_specs=[pl.BlockSpec(memory_space=pl.ANY),
                  pl.BlockSpec((tm, LANE), row), pl.BlockSpec((tm, d), row),
                  pl.BlockSpec((1, d), c), pl.BlockSpec((1, d), c)],
        out_specs=[pl.BlockSpec((tm, d), lambda i, sl: (jnp.minimum(i, npt - 1), 0)),
                   pl.BlockSpec((tm, d), lambda i, sl: (jnp.maximum(i - npt, 0), 0))],
        scratch_shapes=[pltpu.VMEM((2, TOP_K, tm, d), F32), pltpu.SemaphoreType.DMA((2,))])
    return pl.pallas_call(
        functools.partial(_combine_ln_kernel, alpha=alpha, n_prompt_tiles=npt, n_tiles=n_tiles, tm=tm),
        grid_spec=grid_spec,
        out_shape=[jax.ShapeDtypeStruct((tp, d), F32), jax.ShapeDtypeStruct((t - tp, d), F32)],
        compiler_params=_cparams(("arbitrary",)),
        name="combine_ln",
    )(slot_flat, ys, gates, hmid, g, b)


def _rope_tables(pos):
    inv = ROPE_THETA ** (-jnp.arange(0, QK_ROPE, 2, dtype=F32) / QK_ROPE)
    ang = pos.astype(F32)[:, None] * inv[None, :]
    cos, sin = jnp.cos(ang), jnp.sin(ang)
    return jnp.tile(cos, (1, 4)), jnp.tile(jnp.concatenate([-sin, sin], axis=-1), (1, 2))


def _t5_bucket(dist):
    n = jnp.maximum(dist, 0)
    max_exact = N_BUCKETS // 2
    large = max_exact + (jnp.log(jnp.maximum(n, max_exact).astype(F32) / max_exact)
                         / math.log(MAX_DISTANCE / max_exact) * (N_BUCKETS - max_exact)).astype(jnp.int32)
    return jnp.where(n < max_exact, n, jnp.minimum(large, N_BUCKETS - 1))


def _bias_tiles(rb, blk):
    n_dh = rb.shape[1]
    period = 2 * blk
    tiles = []
    for tt in range(2):
        e = jnp.arange(period, dtype=jnp.int32)
        e = jnp.where(e < blk, e, e - period)
        v = rb[_t5_bucket(tt * blk - e)].T
        x = jnp.tile(v, (1, blk))[:, :blk * (period - 1)].reshape(n_dh, blk, period - 1)
        tiles.append(x[:, :, :blk])
    return jnp.stack(tiles)


def _route_layout(ids, n_experts, n_items, n_rows):
    t = ids.shape[0]
    flat = ids.reshape(-1)
    onehot = (flat[:, None] == jnp.arange(n_experts, dtype=jnp.int32)[None, :]).astype(jnp.int32)
    csum = jnp.cumsum(onehot, axis=0)
    counts = csum[-1]
    rank = jnp.take_along_axis(csum, flat[:, None], axis=1)[:, 0] - 1
    nsub_e = (counts + ROW_SUB - 1) // ROW_SUB
    seg_start = (jnp.cumsum(nsub_e) - nsub_e) * ROW_SUB
    slot = seg_start[flat] + rank
    tok_of_row = jnp.zeros((n_rows + ROWS_PER_ITEM,), jnp.int32).at[slot].set(
        jnp.arange(t * TOP_K, dtype=jnp.int32) // TOP_K, unique_indices=True)
    n_used = jnp.sum(nsub_e).astype(jnp.int32).reshape(1)
    sub_per_item = ROWS_PER_ITEM // ROW_SUB
    assert n_items * sub_per_item > n_rows // ROW_SUB + (sub_per_item - 1) * n_experts, "last item must be empty"
    items_e = (nsub_e + sub_per_item - 1) // sub_per_item
    item_end = jnp.cumsum(items_e)
    item_start = item_end - items_e
    w = jnp.arange(n_items, dtype=jnp.int32)
    n_valid = item_end[-1]
    e_of_w = jnp.minimum(jnp.searchsorted(item_end, w, side="right"), n_experts - 1).astype(jnp.int32)
    chunk = w - item_start[e_of_w]
    nsub_w = jnp.clip(nsub_e[e_of_w] - chunk * sub_per_item, 0, sub_per_item)
    valid = w < n_valid
    last_e = e_of_w[jnp.maximum(n_valid - 1, 0)]
    item_expert = jnp.where(valid, e_of_w, last_e).astype(jnp.int32)
    item_nsub = jnp.where(valid, nsub_w, 0).astype(jnp.int32)
    item_row0 = jnp.where(valid, seg_start[e_of_w] + chunk * ROWS_PER_ITEM, 0).astype(jnp.int32)
    return slot.reshape(t, TOP_K), tok_of_row, n_used, item_expert, item_row0, item_nsub


def _act_perm():
    n = np.arange(LANE)
    src = np.where(n < LANE // 2, 2 * n, 2 * (n - LANE // 2) + 1)
    p = np.zeros((LANE, LANE), np.float32)
    p[src, n] = 1.0
    return jnp.asarray(p, BF16)


def kernel(x_prompt, x_sample, cache_ckv, cache_kpe, cache_dk, cache_dv, page_table, rel_bias,
           w_in, g_q, w_uq, g_kv, w_ukv, diff_lambda, g_sub, w_out, ln1_g, ln1_b,
           w_router, b_router, w_gate_up, b_gate_up, w_down, b_down, ln2_g, ln2_b):
    bp, sp, d = x_prompt.shape
    bs, ss, _ = x_sample.shape
    assert ss == 1, "decode path handles one new token per sequence"
    depth = w_in.shape[0]
    n_heads = (d // 2) // V_HEAD
    n_dh = (d // 2) // DIFF_V
    q_lora = w_uq.shape[1]
    n_experts = w_router.shape[2]
    page = cache_ckv.shape[2]
    n_pages = page_table.shape[1]
    past_len = n_pages * page
    tp, ts = bp * sp, bs * ss
    t = tp + ts
    dn_alpha = (2 * depth) ** 0.25

    c_ckv = q_lora
    c_dkv = c_ckv + KV_LORA
    c_qd = c_dkv + 2 * DIFF_V
    c_pe = c_qd + 2 * n_dh * DIFF_HEAD
    n_in = c_pe + LANE
    assert c_ckv % KV_LORA == 0 and c_dkv % (2 * DIFF_V) == 0
    assert (t * TOP_K) % ROW_SUB == 0
    i1 = q_lora
    i2 = i1 + KV_LORA + QK_ROPE
    i3 = i2 + 2 * n_dh * DIFF_HEAD
    i4 = i3 + 2 * DIFF_HEAD

    tm_tok = _pick(t, (640, 320, 256, 128, 64, 32, 16, 8))
    tm_big = _pick(t, (1664, 640, 320, 256, 128, 64, 32, 16, 8))
    blk = _pick(sp, (256, 128))
    gp = _pick(n_pages, (PAGES_PER_STEP, 4, 2, 1))
    assert sp % blk == 0 and blk >= MAX_DISTANCE and page >= MAX_DISTANCE

    pos_p = jnp.arange(sp, dtype=jnp.int32)
    pos_all = jnp.concatenate([jnp.tile(pos_p, bp), jnp.full((ts,), past_len, jnp.int32)])
    cos2, sin2 = _rope_tables(pos_all)

    far_idx = _t5_bucket(jnp.asarray(2 * blk, jnp.int32))
    offs = jnp.arange(page, dtype=jnp.int32)
    last_idx = _t5_bucket(past_len - ((n_pages - 1) * page + offs))
    far_dec_idx = _t5_bucket(jnp.asarray(past_len - ((n_pages - 1) * page - 1), jnp.int32))

    x_all = jnp.concatenate([x_prompt.reshape(tp, d), x_sample.reshape(ts, d)], axis=0)
    perm = _act_perm()
    kpe_t = jnp.swapaxes(cache_kpe, 2, 3)
    new_p, new_s = [], []
    for l in range(depth):
        lam_init = 0.8 - 0.6 * math.exp(-0.3 * l)
        wi = w_in[l]
        w_in2 = jnp.concatenate(
            [wi[:, :i1], wi[:, i1:i1 + KV_LORA], wi[:, i3:i4], wi[:, i4:], wi[:, i2:i3],
             wi[:, i1 + KV_LORA:i2], jnp.zeros((d, LANE - QK_ROPE), wi.dtype)], axis=1).astype(BF16)
        wuq = w_uq[l].reshape(q_lora, n_heads, QK_NOPE + QK_ROPE)
        wuq2 = jnp.concatenate([wuq[:, :, :QK_NOPE].reshape(q_lora, -1),
                                wuq[:, :, QK_NOPE:].reshape(q_lora, -1)], axis=1).astype(BF16)
        wk = jnp.transpose(w_ukv[l][:, :, :QK_NOPE], (1, 2, 0)).astype(BF16)
        wv = jnp.transpose(w_ukv[l][:, :, QK_NOPE:], (1, 0, 2)).astype(BF16)
        wo = w_out[l].astype(BF16)
        half = n_heads * V_HEAD

        x_bf = x_all.astype(BF16)
        h2 = _matmul(x_bf, w_in2, tm_big, _pick(n_in, (384, 256, 128)), F32, "in_proj")
        ckv, kpe, dk, dv, kall, dkv = _kv_post(h2, g_kv[l][None, :], cos2, sin2, q_lora, c_pe, tm_tok)
        q_all = _q_proj(h2, g_q[l][None, :], wuq2, wk, cos2, sin2, q_lora, n_heads,
                        _pick(t, (320, 256, 128, 64, 32, 16, 8)))

        rb = rel_bias.astype(F32)
        far_tile = jnp.broadcast_to(rb[far_idx][None, :, None, None], (1, n_dh, blk, blk))
        bias_tiles = jnp.concatenate([_bias_tiles(rb, blk), far_tile], axis=0)
        dl = diff_lambda[l].astype(F32)
        gs = g_sub[l][None, :].astype(F32)
        mix_mla_p = _mla_prompt(q_all, kall, wv, bp, sp, blk, _pick(n_heads, (4, 2, 1)))
        mix_diff_p = _diff_prompt(h2, dkv, bias_tiles, dl, gs, bp, sp, blk, c_qd, n_dh,
                                  _pick(n_dh, (4, 2, 1)), lam_init)

        qm_s = jnp.transpose(q_all[:, tp:, :], (1, 0, 2))
        qd_s = h2[tp:, c_qd:c_qd + 2 * n_dh * DIFF_HEAD].astype(BF16).reshape(ts, 2 * n_dh, DIFF_HEAD)
        nk = gp * page
        bfar = jnp.broadcast_to(jnp.tile(rb[far_dec_idx], 2)[:, None], (2 * n_dh, nk))
        blast = jnp.concatenate([bfar[:, :nk - page], jnp.tile(rb[last_idx].T, (2, 1))], axis=1)
        bself = jnp.broadcast_to(jnp.tile(rb[0], 2)[:, None], (2 * n_dh, LANE))
        olat, od = _decode(page_table, qm_s, qd_s, kall[tp:, None, :], dkv[tp:, None, :],
                           bfar, blast, bself, cache_ckv, kpe_t, cache_dk, cache_dv, l, gp)
        od_h = jnp.transpose(od.reshape(ts, 2, n_dh, DIFF_V), (1, 2, 0, 3))
        mix_mla_s, mix_diff_s = _decode_merge(jnp.transpose(olat, (1, 0, 2)), wv, od_h, dl, gs, lam_init)

        mix_a = jnp.concatenate([mix_mla_p, mix_mla_s], axis=0)
        mix_b = jnp.concatenate([mix_diff_p, mix_diff_s], axis=0)
        z = _matmul2(mix_a, mix_b, wo[:half], wo[half:], tm_big, _pick(d, (512, 256, 128)), "out_proj")
        hmid, hpk, ids128, gates128 = _ln_router(
            z, x_all, ln1_g[l][None, :], ln1_b[l][None, :], w_router[l].astype(BF16),
            b_router[l][None, :].astype(F32), dn_alpha, _pick(t, (320, 256, 128, 64, 32, 16, 8)))

        n_rows = t * TOP_K + n_experts * ROW_SUB
        spi = ROWS_PER_ITEM // ROW_SUB
        n_items = (n_rows // ROW_SUB + (spi - 1) * n_experts) // spi + 1
        slot, tok_of_row, n_used, item_expert, item_row0, item_nsub = _route_layout(
            ids128[:, :TOP_K], n_experts, n_items, n_rows)
        ys = _moe_ffn(item_expert, item_row0, item_nsub, tok_of_row, n_used, hpk, perm,
                      w_gate_up, b_gate_up, w_down, b_down, l, n_rows)
        out_p, out_s = _combine_ln(slot.reshape(-1), ys, gates128, hmid,
                                   ln2_g[l][None, :], ln2_b[l][None, :], dn_alpha,
                                   _pick(math.gcd(tp, ts), (128, 64, 32, 16, 8)), tp)
        if l + 1 < depth:
            x_all = jnp.concatenate([out_p, out_s], axis=0)
        new_p.append((ckv[:tp].reshape(bp, sp, KV_LORA), kpe[:tp].reshape(bp, sp, QK_ROPE),
                      dk[:tp].reshape(bp, sp, 2, DIFF_HEAD), dv[:tp].reshape(bp, sp, DIFF_V)))
        new_s.append((ckv[tp:].reshape(bs, ss, KV_LORA), kpe[tp:].reshape(bs, ss, QK_ROPE),
                      dk[tp:].reshape(bs, ss, 2, DIFF_HEAD), dv[tp:].reshape(bs, ss, DIFF_V)))

    stack = lambda rows, i: jnp.stack([r_[i] for r_ in rows])
    return (out_p.reshape(bp, sp, d), out_s.reshape(bs, ss, d),
            stack(new_p, 0), stack(new_p, 1), stack(new_p, 2), stack(new_p, 3),
            stack(new_s, 0), stack(new_s, 1), stack(new_s, 2), stack(new_s, 3))
```
